```python
import math
import jax, jax.numpy as jnp
from jax import lax
import numpy as np

D_MODEL = 1024
BATCH = 4
SEQ = 8192
DEPTH = 4
DEC_BATCH = 32
DEC_SEQ = 16
PAST_LEN = 1024

CHUNK = 64
N_MIXERS = 2
N_DN_LAYERS = (DEPTH + N_MIXERS - 1) // N_MIXERS
N_MLA_LAYERS = DEPTH // N_MIXERS
PLE_DIM = 256
D_FF = 4 * D_MODEL
EPS = 1e-6
ALPHA = (2 * DEPTH) ** 0.25
BETA_INIT = (8 * DEPTH) ** -0.25
DN_HEADS = 8
DN_DK = 128
DN_DV = 128
DN_QK_DIM = DN_HEADS * DN_DK
DN_V_DIM = DN_HEADS * DN_DV
DN_CONV_DIM = 2 * DN_QK_DIM + DN_V_DIM
DN_IN_DIM = DN_CONV_DIM + DN_V_DIM + 2 * DN_HEADS
CONV_W = 4
MLA_HEADS = 8
Q_LORA = 512
KV_LORA = 256
NOPE_DIM = 128
ROPE_DIM = 64
V_DIM = 128
MLA_IN_DIM = Q_LORA + KV_LORA + ROPE_DIM
MLA_SCALE = (NOPE_DIM + ROPE_DIM) ** -0.5
ROPE_THETA = 10000.0
Q_BLOCK = 128

kernel_name = 'hybrid_gdn_mla_stream_step'


def rms_norm(x, g):
    xf = x.astype(jnp.float32)
    y = xf * lax.rsqrt(jnp.mean(xf * xf, axis=-1, keepdims=True) + EPS)
    return (y * g.astype(jnp.float32)).astype(x.dtype)


def layer_norm(x, g, b):
    xf = x.astype(jnp.float32)
    xc = xf - jnp.mean(xf, axis=-1, keepdims=True)
    var = jnp.mean(xc * xc, axis=-1, keepdims=True)
    return (xc * lax.rsqrt(var + EPS) * g.astype(jnp.float32) + b.astype(jnp.float32)).astype(x.dtype)


def rope(x, pos):
    half = ROPE_DIM // 2
    inv_freq = ROPE_THETA ** (-jnp.arange(half, dtype=jnp.float32) / half)
    ang = pos.astype(jnp.float32)[:, None] * inv_freq[None, :]
    cos = jnp.cos(ang)[None, :, None, :]
    sin = jnp.sin(ang)[None, :, None, :]
    xf = x.astype(jnp.float32)
    x1, x2 = xf[..., :half], xf[..., half:]
    return jnp.concatenate([x1 * cos - x2 * sin, x2 * cos + x1 * sin], axis=-1).astype(x.dtype)


def causal_dwconv(x_full, w):
    return lax.conv_general_dilated(
        x_full, w[:, None, :].astype(x_full.dtype), window_strides=(1,), padding='VALID',
        dimension_numbers=('NWC', 'WIO', 'NWC'), feature_group_count=x_full.shape[-1])


def chunk_gated_delta(q, k, v, g, beta, s0):
    bsz, seq, nh, _ = q.shape
    c = min(CHUNK, seq)
    n = seq // c

    def blocks(t):
        return jnp.moveaxis(t.reshape((bsz, n, c, nh) + t.shape[3:]), 3, 2)

    q, k, v, g, beta = blocks(q), blocks(k), blocks(v), blocks(g), blocks(beta)
    gc = jnp.cumsum(g, axis=-1)
    idx = jnp.arange(c)
    causal = idx[:, None] >= idx[None, :]
    strict = idx[:, None] > idx[None, :]
    decay = jnp.exp(jnp.where(causal, gc[..., :, None] - gc[..., None, :], -jnp.inf))
    kb = k * beta[..., None]
    a_low = jnp.where(strict, jnp.einsum('bnhid,bnhjd->bnhij', kb, k) * decay, 0.0)
    rhs = jnp.concatenate([v * beta[..., None], kb * jnp.exp(gc)[..., None]], axis=-1)
    uw = lax.linalg.triangular_solve(a_low + jnp.eye(c, dtype=jnp.float32), rhs,
                                     left_side=True, lower=True, unit_diagonal=True)
    u, w = uw[..., :DN_DV], uw[..., DN_DV:]
    attn = jnp.einsum('bnhid,bnhjd->bnhij', q, k) * decay
    qg = q * jnp.exp(gc)[..., None]
    g_last = gc[..., -1]
    kd = k * jnp.exp(g_last[..., None] - gc)[..., None]

    def step(s, xs):
        u_c, w_c, qg_c, kd_c, attn_c, gl_c = xs
        v_new = u_c - jnp.einsum('bhcd,bhde->bhce', w_c, s)
        o = jnp.einsum('bhcd,bhde->bhce', qg_c, s) + jnp.einsum('bhij,bhje->bhie', attn_c, v_new)
        s = s * jnp.exp(gl_c)[..., None, None] + jnp.einsum('bhcd,bhce->bhde', kd_c, v_new)
        return s, o

    xs = tuple(jnp.moveaxis(t, 1, 0) for t in (u, w, qg, kd, attn, g_last))
    s_fin, o = lax.scan(step, s0, xs)
    o = jnp.transpose(o, (1, 0, 3, 2, 4)).reshape(bsz, seq, nh, DN_DV)
    return o, s_fin


def gated_deltanet(x, conv_prev, s0, w_in, conv_w, a_log, dt_bias, o_norm, w_o):
    bsz, seq, _ = x.shape
    proj = x @ w_in
    qkv = proj[..., :DN_CONV_DIM]
    z = proj[..., DN_CONV_DIM:DN_CONV_DIM + DN_V_DIM]
    b = proj[..., DN_CONV_DIM + DN_V_DIM:DN_CONV_DIM + DN_V_DIM + DN_HEADS]
    a = proj[..., DN_CONV_DIM + DN_V_DIM + DN_HEADS:]
    qkv_full = jnp.concatenate([conv_prev.astype(x.dtype), qkv], axis=1)
    new_conv = qkv_full[:, -(CONV_W - 1):]
    qkv = jax.nn.silu(causal_dwconv(qkv_full, conv_w)).astype(jnp.float32)
    q = qkv[..., :DN_QK_DIM].reshape(bsz, seq, DN_HEADS, DN_DK)
    k = qkv[..., DN_QK_DIM:2 * DN_QK_DIM].reshape(bsz, seq, DN_HEADS, DN_DK)
    v = qkv[..., 2 * DN_QK_DIM:].reshape(bsz, seq, DN_HEADS, DN_DV)
    q = q * lax.rsqrt(jnp.sum(q * q, axis=-1, keepdims=True) + EPS) * (DN_DK ** -0.5)
    k = k * lax.rsqrt(jnp.sum(k * k, axis=-1, keepdims=True) + EPS)
    beta = jax.nn.sigmoid(b.astype(jnp.float32))
    g = -jnp.exp(a_log.astype(jnp.float32)) * jax.nn.softplus(a.astype(jnp.float32) + dt_bias.astype(jnp.float32))
    o, s_fin = chunk_gated_delta(q, k, v, g, beta, s0.astype(jnp.float32))
    o = o * lax.rsqrt(jnp.mean(o * o, axis=-1, keepdims=True) + EPS) * o_norm.astype(jnp.float32)
    o = o * jax.nn.silu(z.astype(jnp.float32)).reshape(bsz, seq, DN_HEADS, DN_DV)
    out = o.reshape(bsz, seq, DN_V_DIM).astype(x.dtype) @ w_o
    return out, new_conv, s_fin.astype(x.dtype)


def mla_project(x, pos, w_in, q_norm, w_uq, kv_norm):
    bsz, seq, _ = x.shape
    proj = x @ w_in
    c_q = rms_norm(proj[..., :Q_LORA], q_norm)
    c_kv = rms_norm(proj[..., Q_LORA:Q_LORA + KV_LORA], kv_norm)
    k_rope = rope(proj[..., Q_LORA + KV_LORA:][:, :, None, :], pos)[:, :, 0, :]
    q = (c_q @ w_uq).reshape(bsz, seq, MLA_HEADS, NOPE_DIM + ROPE_DIM)
    return q[..., :NOPE_DIM], rope(q[..., NOPE_DIM:], pos), c_kv, k_rope


def chunk_mask(q_pos, k_pos):
    return (k_pos[None, :] // CHUNK) <= (q_pos[:, None] // CHUNK)


def mla_prompt(x, w_in, q_norm, w_uq, kv_norm, w_uk, w_uv, w_o):
    bsz, seq, _ = x.shape
    pos = jnp.arange(seq)
    q_nope, q_rope, c_kv, k_rope = mla_project(x, pos, w_in, q_norm, w_uq, kv_norm)
    k_nope = jnp.einsum('bsc,chd->bshd', c_kv, w_uk)
    v = jnp.einsum('bsc,chd->bshd', c_kv, w_uv)
    qb = min(Q_BLOCK, seq)

    def attend_block(start):
        qn = lax.dynamic_slice_in_dim(q_nope, start, qb, axis=1)
        qr = lax.dynamic_slice_in_dim(q_rope, start, qb, axis=1)
        s = jnp.einsum('bqhd,bkhd->bhqk', qn, k_nope) + jnp.einsum('bqhd,bkd->bhqk', qr, k_rope)
        s = jnp.where(chunk_mask(start + jnp.arange(qb), pos), s.astype(jnp.float32) * MLA_SCALE, -jnp.inf)
        p = jax.nn.softmax(s, axis=-1).astype(v.dtype)
        return jnp.einsum('bhqk,bkhd->bqhd', p, v)

    o = lax.map(attend_block, jnp.arange(seq // qb) * qb)
    o = jnp.moveaxis(o, 0, 1).reshape(bsz, seq, MLA_HEADS * V_DIM)
    return o @ w_o, c_kv, k_rope


def mla_sample(x, ckv_cache, kr_cache, w_in, q_norm, w_uq, kv_norm, w_uk, w_uv, w_o):
    bsz, seq, _ = x.shape
    past = ckv_cache.shape[1]
    q_pos = past + jnp.arange(seq)
    q_nope, q_rope, c_kv, k_rope = mla_project(x, q_pos, w_in, q_norm, w_uq, kv_norm)
    ckv_all = jnp.concatenate([ckv_cache.astype(x.dtype), c_kv], axis=1)
    kr_all = jnp.concatenate([kr_cache.astype(x.dtype), k_rope], axis=1)
    q_lat = jnp.einsum('bqhd,chd->bqhc', q_nope, w_uk)
    s = jnp.einsum('bqhc,bkc->bhqk', q_lat, ckv_all) + jnp.einsum('bqhd,bkd->bhqk', q_rope, kr_all)
    s = jnp.where(chunk_mask(q_pos, jnp.arange(past + seq)), s.astype(jnp.float32) * MLA_SCALE, -jnp.inf)
    p = jax.nn.softmax(s, axis=-1).astype(x.dtype)
    o_lat = jnp.einsum('bhqk,bkc->bqhc', p, ckv_all)
    o = jnp.einsum('bqhc,chd->bqhd', o_lat, w_uv).reshape(bsz, seq, MLA_HEADS * V_DIM)
    return o @ w_o, c_kv, k_rope


def finish_layer(y, m, p, ln1_g, ln1_b, w_up, w_down, ln2_g, ln2_b, ple_proj, ple_norm, ple_gate):
    y = layer_norm(ALPHA * y + m, ln1_g, ln1_b)
    h = jnp.square(jax.nn.relu(y @ w_up))
    y = layer_norm(ALPHA * y + h @ w_down, ln2_g, ln2_b)
    e = rms_norm(p @ ple_proj, ple_norm)
    return y + jax.nn.sigmoid(y @ ple_gate) * e


def setup_inputs(seed: int = 0) -> dict:
    key = jax.random.key(seed)
    ks = iter(jax.random.split(key, 40))
    f32 = jnp.float32

    def nrm(shape, scale=1.0):
        return jax.random.normal(next(ks), shape, f32) * scale

    def gain(shape):
        return 1.0 + nrm(shape, 0.02)

    def unif(shape, lo, hi):
        return jax.random.uniform(next(ks), shape, f32, lo, hi)

    dt = jnp.exp(unif((N_DN_LAYERS, DN_HEADS), math.log(1e-3), math.log(1e-1)))
    return {
        'x_prompt': nrm((BATCH, SEQ, D_MODEL)),
        'x_sample': nrm((DEC_BATCH, DEC_SEQ, D_MODEL)),
        'state_dn_conv': nrm((N_DN_LAYERS, DEC_BATCH, CONV_W - 1, DN_CONV_DIM)),
        'state_dn_recurrent': nrm((N_DN_LAYERS, DEC_BATCH, DN_HEADS, DN_DK, DN_DV), 0.1),
        'cache_mla_ckv': nrm((N_MLA_LAYERS, DEC_BATCH, PAST_LEN, KV_LORA)),
        'cache_mla_krope': nrm((N_MLA_LAYERS, DEC_BATCH, PAST_LEN, ROPE_DIM)),
        'p_prompt': nrm((DEPTH, BATCH, SEQ, PLE_DIM)),
        'p_sample': nrm((DEPTH, DEC_BATCH, DEC_SEQ, PLE_DIM)),
        'ln1_g': gain((DEPTH, D_MODEL)),
        'ln1_b': nrm((DEPTH, D_MODEL), 0.02),
        'ln2_g': gain((DEPTH, D_MODEL)),
        'ln2_b': nrm((DEPTH, D_MODEL), 0.02),
        'mlp_w_up': nrm((DEPTH, D_MODEL, D_FF), D_MODEL ** -0.5),
        'mlp_w_down': nrm((DEPTH, D_FF, D_MODEL), D_FF ** -0.5 * BETA_INIT),
        'ple_w_proj': nrm((DEPTH, PLE_DIM, D_MODEL), PLE_DIM ** -0.5),
        'ple_norm': gain((DEPTH, D_MODEL)),
        'ple_w_gate': nrm((DEPTH, D_MODEL, D_MODEL), D_MODEL ** -0.5),
        'dn_w_in': nrm((N_DN_LAYERS, D_MODEL, DN_IN_DIM), D_MODEL ** -0.5),
        'dn_conv_w': nrm((N_DN_LAYERS, CONV_W, DN_CONV_DIM), CONV_W ** -0.5),
        'dn_a_log': jnp.log(unif((N_DN_LAYERS, DN_HEADS), 1.0, 16.0)),
        'dn_dt_bias': dt + jnp.log(-jnp.expm1(-dt)),
        'dn_o_norm': gain((N_DN_LAYERS, DN_DV)),
        'dn_w_o': nrm((N_DN_LAYERS, DN_V_DIM, D_MODEL), DN_V_DIM ** -0.5 * BETA_INIT),
        'mla_w_in': nrm((N_MLA_LAYERS, D_MODEL, MLA_IN_DIM), D_MODEL ** -0.5),
        'mla_q_norm': gain((N_MLA_LAYERS, Q_LORA)),
        'mla_w_uq': nrm((N_MLA_LAYERS, Q_LORA, MLA_HEADS * (NOPE_DIM + ROPE_DIM)), Q_LORA ** -0.5),
        'mla_kv_norm': gain((N_MLA_LAYERS, KV_LORA)),
        'mla_w_uk': nrm((N_MLA_LAYERS, KV_LORA, MLA_HEADS, NOPE_DIM), KV_LORA ** -0.5),
        'mla_w_uv': nrm((N_MLA_LAYERS, KV_LORA, MLA_HEADS, V_DIM), KV_LORA ** -0.5),
        'mla_w_o': nrm((N_MLA_LAYERS, MLA_HEADS * V_DIM, D_MODEL), (MLA_HEADS * V_DIM) ** -0.5 * BETA_INIT),
    }


def reference(x_prompt, x_sample, state_dn_conv, state_dn_recurrent, cache_mla_ckv, cache_mla_krope,
              p_prompt, p_sample, ln1_g, ln1_b, ln2_g, ln2_b, mlp_w_up, mlp_w_down,
              ple_w_proj, ple_norm, ple_w_gate, dn_w_in, dn_conv_w, dn_a_log, dn_dt_bias, dn_o_norm, dn_w_o,
              mla_w_in, mla_q_norm, mla_w_uq, mla_kv_norm, mla_w_uk, mla_w_uv, mla_w_o):
    yp, ys = x_prompt, x_sample
    bp = x_prompt.shape[0]
    p_conv, p_rec, p_ckv, p_kr = [], [], [], []
    s_conv, s_rec, s_ckv, s_kr = [], [], [], []
    for i in range(DEPTH):
        j = i // N_MIXERS
        if i % N_MIXERS == 0:
            zero_conv = jnp.zeros((bp, CONV_W - 1, DN_CONV_DIM), x_prompt.dtype)
            zero_state = jnp.zeros((bp, DN_HEADS, DN_DK, DN_DV), x_prompt.dtype)
            mp, c, s = gated_deltanet(yp, zero_conv, zero_state, dn_w_in[j], dn_conv_w[j], dn_a_log[j],
                                      dn_dt_bias[j], dn_o_norm[j], dn_w_o[j])
            p_conv.append(c)
            p_rec.append(s)
            ms, c, s = gated_deltanet(ys, state_dn_conv[j], state_dn_recurrent[j], dn_w_in[j], dn_conv_w[j],
                                      dn_a_log[j], dn_dt_bias[j], dn_o_norm[j], dn_w_o[j])
            s_conv.append(c)
            s_rec.append(s)
        else:
            mp, c, kr = mla_prompt(yp, mla_w_in[j], mla_q_norm[j], mla_w_uq[j], mla_kv_norm[j],
                                   mla_w_uk[j], mla_w_uv[j], mla_w_o[j])
            p_ckv.append(c)
            p_kr.append(kr)
            ms, c, kr = mla_sample(ys, cache_mla_ckv[j], cache_mla_krope[j], mla_w_in[j], mla_q_norm[j],
                                   mla_w_uq[j], mla_kv_norm[j], mla_w_uk[j], mla_w_uv[j], mla_w_o[j])
            s_ckv.append(c)
            s_kr.append(kr)
        yp = finish_layer(yp, mp, p_prompt[i], ln1_g[i], ln1_b[i], mlp_w_up[i], mlp_w_down[i],
                          ln2_g[i], ln2_b[i], ple_w_proj[i], ple_norm[i], ple_w_gate[i])
        ys = finish_layer(ys, ms, p_sample[i], ln1_g[i], ln1_b[i], mlp_w_up[i], mlp_w_down[i],
                          ln2_g[i], ln2_b[i], ple_w_proj[i], ple_norm[i], ple_w_gate[i])
    prompt_dn_conv = jnp.stack(p_conv)
    prompt_dn_recurrent = jnp.stack(p_rec)
    prompt_mla_ckv = jnp.stack(p_ckv)
    prompt_mla_krope = jnp.stack(p_kr)
    sample_dn_conv = jnp.stack(s_conv)
    sample_dn_recurrent = jnp.stack(s_rec)
    sample_mla_ckv = jnp.stack(s_ckv)
    sample_mla_krope = jnp.stack(s_kr)
    return (yp, ys, prompt_dn_conv, prompt_dn_recurrent, prompt_mla_ckv, prompt_mla_krope,
            sample_dn_conv, sample_dn_recurrent, sample_mla_ckv, sample_mla_krope)
```

```python
import functools
import math

import jax
import jax.numpy as jnp
from jax import lax
from jax.experimental import pallas as pl
from jax.experimental.pallas import tpu as pltpu

F32 = jnp.float32
BF16 = jnp.bfloat16

CHUNK = 64
N_MIXERS = 2
EPS = 1e-6
DN_HEADS = 8
DN_DK = 128
DN_DV = 128
CONV_W = 4
MLA_HEADS = 8
Q_LORA = 512
KV_LORA = 256
NOPE_DIM = 128
ROPE_DIM = 64
V_DIM = 128
MLA_SCALE = (NOPE_DIM + ROPE_DIM) ** -0.5
ROPE_THETA = 10000.0

LANES = 128
SUBLANES = 8
VMEM_BUDGET_BYTES = 60000 * 1024

TOKEN_TILE = 512
FF_CHUNK = 1024
DN_STEP_TOKENS = 256
ATTN_TILE = 512
QK_PAD = 2 * LANES


def _vmem_limit(nbytes):
    return int(min(VMEM_BUDGET_BYTES, nbytes))


def _nbytes(shape, dtype):
    return math.prod(shape) * jnp.dtype(dtype).itemsize


def _resident(shape, index_map):
    return pl.BlockSpec(shape, index_map, pipeline_mode=pl.Buffered(1))


def _mm(a, b):
    return jnp.dot(a.astype(BF16), b.astype(BF16), preferred_element_type=F32)


def _mm_nt(a, b):
    return lax.dot_general(a.astype(BF16), b.astype(BF16), (((1,), (1,)), ((), ())),
                           preferred_element_type=F32)


def _mm_tn(a, b):
    return pl.dot(a.astype(BF16), b.astype(BF16), trans_a=True)


def _split2(x):
    hi = x.astype(BF16)
    lo = (x - hi.astype(F32)).astype(BF16)
    return hi, lo


def _mm3(a, b):
    ah, al = _split2(a)
    bh, bl = _split2(b)
    return _mm(ah, bh) + (_mm(ah, bl) + _mm(al, bh))


def _mm_exact_lhs3(a_bf16, b):
    b0 = b.astype(BF16)
    r1 = b - b0.astype(F32)
    b1 = r1.astype(BF16)
    b2 = (r1 - b1.astype(F32)).astype(BF16)
    return _mm(a_bf16, b0) + (_mm(a_bf16, b1) + _mm(a_bf16, b2))


def _layer_norm(x, g, b):
    xc = x - jnp.mean(x, axis=-1, keepdims=True)
    var = jnp.mean(xc * xc, axis=-1, keepdims=True)
    return xc * lax.rsqrt(var + EPS) * g + b


def _rms_norm(x, g):
    return x * lax.rsqrt(jnp.mean(x * x, axis=-1, keepdims=True) + EPS) * g


def _sigmoid(x):
    return 1.0 / (1.0 + jnp.exp(-x))


def _softplus(x):
    return jnp.maximum(x, 0.0) + jnp.log1p(jnp.exp(-jnp.abs(x)))


def _finish_kernel(y_ref, o_ref, p_ref, wo_ref, ln1g_ref, ln1b_ref, wup_ref, wdn_ref, ln2g_ref, ln2b_ref,
                   wpp_ref, pn_ref, wg_ref, out_ref, *, alpha, d_ff):
    y = y_ref[...]
    m = _mm(o_ref[...], wo_ref[...])
    y1 = _layer_norm(alpha * y + m, ln1g_ref[...], ln1b_ref[...])
    y1b = y1.astype(BF16)
    acc = jnp.zeros_like(y1)
    for c in range(d_ff // FF_CHUNK):
        h = _mm(y1b, wup_ref[:, c * FF_CHUNK:(c + 1) * FF_CHUNK])
        h = jnp.square(jnp.maximum(h, 0.0))
        acc = acc + _mm(h, wdn_ref[c * FF_CHUNK:(c + 1) * FF_CHUNK, :])
    y2 = _layer_norm(alpha * y1 + acc, ln2g_ref[...], ln2b_ref[...])
    e = _rms_norm(_mm(p_ref[...], wpp_ref[...]), pn_ref[...])
    gate = _sigmoid(_mm(y2, wg_ref[...]))
    out_ref[...] = y2 + gate * e


def _finish_layer(y, o, p, wo, ln1g, ln1b, wup, wdn, ln2g, ln2b, wpp, pn, wg, *, alpha, tm):
    t, d = y.shape
    d_ff = wup.shape[1]
    ple = p.shape[1]
    row = lambda i: (i, 0)
    fix = lambda i: (0, 0)
    weights = 2 * (d * d * 2 + d * d_ff + d_ff * d + ple * d)
    tiles = 2 * (2 * tm * d * 4 + tm * d * 2 + tm * ple * 4)
    temps = 6 * tm * d * 4 + 2 * tm * FF_CHUNK * 4
    return pl.pallas_call(
        functools.partial(_finish_kernel, alpha=alpha, d_ff=d_ff),
        grid=(t // tm,),
        in_specs=[
            pl.BlockSpec((tm, d), row), pl.BlockSpec((tm, d), row), pl.BlockSpec((tm, ple), row),
            _resident((d, d), fix), _resident((1, d), fix), _resident((1, d), fix),
            _resident((d, d_ff), fix), _resident((d_ff, d), fix), _resident((1, d), fix), _resident((1, d), fix),
            _resident((ple, d), fix), _resident((1, d), fix), _resident((d, d), fix),
        ],
        out_specs=pl.BlockSpec((tm, d), row),
        out_shape=jax.ShapeDtypeStruct((t, d), F32),
        compiler_params=pltpu.CompilerParams(dimension_semantics=("arbitrary",),
                                             vmem_limit_bytes=_vmem_limit(weights + tiles + temps)),
        name="finish_layer",
    )(y, o, p, wo, ln1g, ln1b, wup, wdn, ln2g, ln2b, wpp, pn, wg)


def _dn_proj_kernel(y_ref, wqkv_ref, wz_ref, wba_ref, qkv_ref, z_ref, ba_ref):
    yb = y_ref[...].astype(BF16)
    qkv_ref[...] = _mm(yb, wqkv_ref[...])
    z_ref[...] = _mm(yb, wz_ref[...])
    ba_ref[...] = _mm(yb, wba_ref[...])


def _dn_proj(y, wqkv, wz, wba, *, tm):
    t, d = y.shape
    nq, nz, nb = wqkv.shape[1], wz.shape[1], wba.shape[1]
    row = lambda i: (i, 0)
    fix = lambda i: (0, 0)
    weights = 2 * d * (nq + nz + nb)
    tiles = 2 * 4 * tm * (d + nq + nz + nb)
    temps = 2 * tm * d + 4 * tm * nq
    return pl.pallas_call(
        _dn_proj_kernel,
        grid=(t // tm,),
        in_specs=[pl.BlockSpec((tm, d), row), _resident((d, nq), fix), _resident((d, nz), fix),
                  _resident((d, nb), fix)],
        out_specs=[pl.BlockSpec((tm, nq), row), pl.BlockSpec((tm, nz), row), pl.BlockSpec((tm, nb), row)],
        out_shape=[jax.ShapeDtypeStruct((t, nq), F32), jax.ShapeDtypeStruct((t, nz), F32),
                   jax.ShapeDtypeStruct((t, nb), F32)],
        compiler_params=pltpu.CompilerParams(dimension_semantics=("arbitrary",),
                                             vmem_limit_bytes=_vmem_limit(weights + tiles + temps)),
        name="dn_proj",
    )(y, wqkv, wz, wba)


def _dn_core_kernel(qkv_ref, z_ref, ba_ref, cprev_ref, s0_ref, cw_ref, alog_ref, dtb_ref, onorm_ref,
                    o_ref, ctail_ref, sout_ref,
                    xs_ref, q_ref, k_ref, v_ref, s_ref, *, tb, c, n_steps):
    heads = DN_HEADS
    dk = DN_DK
    qk_dim = heads * dk
    g_heads = LANES // c
    n_groups = heads // g_heads
    step = pl.program_id(1)

    @pl.when(step == 0)
    def _():
        xs_ref[0:SUBLANES, :] = cprev_ref[...]
        s_ref[...] = s0_ref[...]

    x = qkv_ref[...]
    xs_ref[SUBLANES:SUBLANES + tb, :] = x
    cw = cw_ref[...]
    conv = x * cw[CONV_W - 1:CONV_W, :]
    for k in range(1, CONV_W):
        conv = conv + xs_ref[SUBLANES - k:SUBLANES - k + tb, :] * cw[CONV_W - 1 - k:CONV_W - k, :]
    tail = xs_ref[tb:tb + SUBLANES, :]
    xs_ref[0:SUBLANES, :] = tail

    @pl.when(step == n_steps - 1)
    def _():
        ctail_ref[...] = tail

    act = conv * _sigmoid(conv)
    for h in range(heads):
        qh = act[:, h * dk:(h + 1) * dk]
        kh = act[:, qk_dim + h * dk:qk_dim + (h + 1) * dk]
        q_ref[:, h * dk:(h + 1) * dk] = qh * lax.rsqrt(jnp.sum(qh * qh, axis=-1, keepdims=True) + EPS) * (dk ** -0.5)
        k_ref[:, h * dk:(h + 1) * dk] = kh * lax.rsqrt(jnp.sum(kh * kh, axis=-1, keepdims=True) + EPS)
    v_ref[...] = act[:, 2 * qk_dim:]

    rows = lax.broadcasted_iota(jnp.int32, (LANES, LANES), 0)
    cols = lax.broadcasted_iota(jnp.int32, (LANES, LANES), 1)
    same_block = (rows // c) == (cols // c)
    causal = same_block & (rows >= cols)
    strict = same_block & (rows > cols)
    eye = (rows == cols).astype(F32)
    lr = lax.broadcasted_iota(jnp.int32, (LANES, c), 0)
    lc = lax.broadcasted_iota(jnp.int32, (LANES, c), 1)
    cum_op = ((lc <= lr) & (lr < c)).astype(BF16)
    neg_exp_alog = -jnp.exp(alog_ref[...])
    dtb = dtb_ref[...]
    onorm = onorm_ref[...]

    def stack(pieces):
        return pieces[0] if len(pieces) == 1 else jnp.concatenate(pieces, axis=0)

    def chunk_body(ci, carry):
        r0 = pl.multiple_of(ci * c, c)
        ba = ba_ref[pl.ds(r0, c), :]
        beta_all = _sigmoid(ba)
        g_all = neg_exp_alog * _softplus(ba + dtb)
        gc = _mm_exact_lhs3(cum_op, g_all)
        gct = gc.T
        exp_gc = jnp.exp(gc)
        for p in range(n_groups):
            hs = [p * g_heads + s for s in range(g_heads)]
            col_gc = stack([jnp.broadcast_to(gc[0:c, heads + h:heads + h + 1], (c, LANES)) for h in hs])
            col_eg = stack([jnp.broadcast_to(exp_gc[0:c, heads + h:heads + h + 1], (c, LANES)) for h in hs])
            col_gl = stack([jnp.broadcast_to(gc[c - 1:c, heads + h:heads + h + 1], (c, LANES)) for h in hs])
            col_beta = stack([jnp.broadcast_to(beta_all[:, h:h + 1], (c, LANES)) for h in hs])
            row_gc = gct[heads + hs[0]:heads + hs[0] + 1, :]
            for s in range(1, g_heads):
                row_gc = row_gc + pltpu.roll(gct[heads + hs[s]:heads + hs[s] + 1, :], s * c, 1)
            decay = jnp.exp(jnp.where(causal, col_gc - row_gc, -jnp.inf))
            qp = stack([q_ref[pl.ds(r0, c), h * dk:(h + 1) * dk] for h in hs])
            kp = stack([k_ref[pl.ds(r0, c), h * dk:(h + 1) * dk] for h in hs])
            vp = stack([v_ref[pl.ds(r0, c), h * dk:(h + 1) * dk] for h in hs])
            kb = kp * col_beta
            a_low = jnp.where(strict, _mm_nt(kb, kp) * decay, 0.0)
            t_inv = eye - a_low
            pw = a_low
            for _ in range(int(math.log2(c)) - 1):
                pw = _mm3(pw, pw)
                t_inv = t_inv + _mm3(t_inv, pw)
            rhs = jnp.concatenate([vp * col_beta, kb * col_eg], axis=1)
            uw = _mm3(t_inv, rhs)
            u = uw[:, :DN_DV]
            w = uw[:, DN_DV:]
            attn = _mm_nt(qp, kp) * decay
            qg = qp * col_eg
            kd = kp * jnp.exp(col_gl - col_gc)
            ws, qs = [], []
            for s, h in enumerate(hs):
                sb = s_ref[h].astype(BF16)
                ws.append(_mm(w[s * c:(s + 1) * c], sb))
                qs.append(_mm(qg[s * c:(s + 1) * c], sb))
            v_new = u - stack(ws)
            o = stack(qs) + _mm(attn, v_new)
            for s, h in enumerate(hs):
                in_block = (rows // c) == s
                kd_s = jnp.where(in_block, kd, 0.0)
                decay_last = jnp.exp(gc[c - 1:c, heads + h:heads + h + 1])
                s_ref[h] = s_ref[h] * decay_last + _mm_tn(kd_s, v_new)
            o = o * lax.rsqrt(jnp.mean(o * o, axis=-1, keepdims=True) + EPS) * onorm
            for s, h in enumerate(hs):
                zh = z_ref[pl.ds(r0, c), h * DN_DV:(h + 1) * DN_DV]
                o_ref[pl.ds(r0, c), h * DN_DV:(h + 1) * DN_DV] = (
                    o[s * c:(s + 1) * c] * (zh * _sigmoid(zh))).astype(o_ref.dtype)
        return carry

    lax.fori_loop(0, tb // c, chunk_body, 0)

    @pl.when(step == n_steps - 1)
    def _():
        sout_ref[...] = s_ref[...]


def _dn_core(qkv, z, ba, conv_prev, s0, conv_w, alog, dtb, onorm, *, n_seq, seq_len, row0, tb, c):
    heads = DN_HEADS
    n_steps = seq_len // tb
    blk0 = row0 // tb
    cdim = qkv.shape[1]
    vdim = z.shape[1]
    tok = lambda b, l: (blk0 + b * n_steps + l, 0)
    seq3 = lambda b, l: (b, 0, 0)
    seq4 = lambda b, l: (b, 0, 0, 0)
    fix = lambda b, l: (0, 0)
    tiles = 2 * tb * (4 * cdim + 4 * vdim + 4 * LANES + 2 * vdim)
    state = 5 * heads * DN_DK * DN_DV * 4 + 4 * SUBLANES * cdim * 4
    scratch = (tb + SUBLANES) * cdim * 4 + 3 * tb * vdim * 4
    temps = 3 * tb * cdim * 4
    return pl.pallas_call(
        functools.partial(_dn_core_kernel, tb=tb, c=c, n_steps=n_steps),
        grid=(n_seq, n_steps),
        in_specs=[
            pl.BlockSpec((tb, cdim), tok), pl.BlockSpec((tb, vdim), tok), pl.BlockSpec((tb, LANES), tok),
            pl.BlockSpec((None, SUBLANES, cdim), seq3), pl.BlockSpec((None, heads, DN_DK, DN_DV), seq4),
            _resident((SUBLANES, cdim), fix), _resident((1, LANES), fix), _resident((1, LANES), fix),
            _resident((1, LANES), fix),
        ],
        out_specs=[
            pl.BlockSpec((tb, vdim), lambda b, l: (b * n_steps + l, 0)),
            pl.BlockSpec((None, SUBLANES, cdim), seq3),
            pl.BlockSpec((None, heads, DN_DK, DN_DV), seq4),
        ],
        out_shape=[
            jax.ShapeDtypeStruct((n_seq * seq_len, vdim), BF16),
            jax.ShapeDtypeStruct((n_seq, SUBLANES, cdim), F32),
            jax.ShapeDtypeStruct((n_seq, heads, DN_DK, DN_DV), F32),
        ],
        scratch_shapes=[
            pltpu.VMEM((tb + SUBLANES, cdim), F32),
            pltpu.VMEM((tb, vdim), F32), pltpu.VMEM((tb, vdim), F32), pltpu.VMEM((tb, vdim), F32),
            pltpu.VMEM((heads, DN_DK, DN_DV), F32),
        ],
        compiler_params=pltpu.CompilerParams(dimension_semantics=("arbitrary", "arbitrary"),
                                             vmem_limit_bytes=_vmem_limit(tiles + state + scratch + temps)),
        name="dn_core",
    )(qkv, z, ba, conv_prev, s0, conv_w, alog, dtb, onorm)


def _rope128(x, cos, sin_lo, sin_hi):
    half = ROPE_DIM // 2
    return x * cos + pltpu.roll(x, LANES - half, 1) * sin_lo + pltpu.roll(x, half, 1) * sin_hi


def _mla_proj_kernel(y_ref, win_ref, qn_ref, kvn_ref, wqn_ref, wqr_ref, wuk_ref, wuv_ref,
                     cos_ref, sl_ref, sh_ref, q_ref, k_ref, v_ref, ckv_ref, kr_ref):
    heads = MLA_HEADS
    proj = _mm(y_ref[...], win_ref[...])
    cq = _rms_norm(proj[:, :Q_LORA], qn_ref[...]).astype(BF16)
    ckv = _rms_norm(proj[:, Q_LORA:Q_LORA + KV_LORA], kvn_ref[...])
    cos, sl, sh = cos_ref[...], sl_ref[...], sh_ref[...]
    kr = _rope128(proj[:, Q_LORA + KV_LORA:], cos, sl, sh)
    ckv_ref[...] = ckv
    kr_ref[...] = kr[:, :ROPE_DIM]
    ckvb = ckv.astype(BF16)
    qn = _mm(cq, wqn_ref[...])
    qr = _mm(cq, wqr_ref[...])
    kn = _mm(ckvb, wuk_ref[...])
    v_ref[...] = _mm(ckvb, wuv_ref[...]).astype(BF16)
    krb = kr.astype(BF16)
    for h in range(heads):
        lo = h * QK_PAD
        q_ref[:, lo:lo + LANES] = qn[:, h * LANES:(h + 1) * LANES].astype(BF16)
        q_ref[:, lo + LANES:lo + QK_PAD] = _rope128(qr[:, h * LANES:(h + 1) * LANES], cos, sl, sh).astype(BF16)
        k_ref[:, lo:lo + LANES] = kn[:, h * LANES:(h + 1) * LANES].astype(BF16)
        k_ref[:, lo + LANES:lo + QK_PAD] = krb


def _mla_proj(y, win, qn, kvn, wqn, wqr, wuk, wuv, cos, sl, sh, *, tm):
    t, d = y.shape
    heads = MLA_HEADS
    nin = win.shape[1]
    row = lambda i: (i, 0)
    fix = lambda i: (0, 0)
    weights = 2 * (d * nin + Q_LORA * 2 * heads * LANES + KV_LORA * 2 * heads * LANES)
    tiles = 2 * tm * (4 * d + 3 * 4 * LANES + 2 * 2 * heads * QK_PAD + 2 * heads * V_DIM + 4 * KV_LORA + 4 * ROPE_DIM)
    temps = tm * (4 * nin + 4 * 4 * heads * LANES + 2 * d)
    return pl.pallas_call(
        _mla_proj_kernel,
        grid=(t // tm,),
        in_specs=[
            pl.BlockSpec((tm, d), row), _resident((d, nin), fix), _resident((1, Q_LORA), fix),
            _resident((1, KV_LORA), fix), _resident((Q_LORA, heads * LANES), fix),
            _resident((Q_LORA, heads * LANES), fix), _resident((KV_LORA, heads * NOPE_DIM), fix),
            _resident((KV_LORA, heads * V_DIM), fix),
            pl.BlockSpec((tm, LANES), row), pl.BlockSpec((tm, LANES), row), pl.BlockSpec((tm, LANES), row),
        ],
        out_specs=[
            pl.BlockSpec((tm, heads * QK_PAD), row), pl.BlockSpec((tm, heads * QK_PAD), row),
            pl.BlockSpec((tm, heads * V_DIM), row), pl.BlockSpec((tm, KV_LORA), row),
            pl.BlockSpec((tm, ROPE_DIM), row),
        ],
        out_shape=[
            jax.ShapeDtypeStruct((t, heads * QK_PAD), BF16), jax.ShapeDtypeStruct((t, heads * QK_PAD), BF16),
            jax.ShapeDtypeStruct((t, heads * V_DIM), BF16), jax.ShapeDtypeStruct((t, KV_LORA), F32),
            jax.ShapeDtypeStruct((t, ROPE_DIM), F32),
        ],
        compiler_params=pltpu.CompilerParams(dimension_semantics=("arbitrary",),
                                             vmem_limit_bytes=_vmem_limit(weights + tiles + temps)),
        name="mla_proj",
    )(y, win, qn, kvn, wqn, wqr, wuk, wuv, cos, sl, sh)


def _flash_kernel(q_ref, k_ref, v_ref, o_ref, m_ref, l_ref, acc_ref, *, tile):
    i = pl.program_id(2)
    q = q_ref[...]
    m_ref[...] = jnp.full(m_ref.shape, -jnp.inf, F32)
    l_ref[...] = jnp.zeros(l_ref.shape, F32)
    acc_ref[...] = jnp.zeros(acc_ref.shape, F32)

    def update(s, vj):
        m_prev = m_ref[...]
        m_new = jnp.maximum(m_prev, jnp.max(s, axis=-1, keepdims=True))
        alpha = jnp.exp(m_prev - m_new)
        p = jnp.exp(s - m_new)
        l_ref[...] = alpha * l_ref[...] + jnp.sum(p, axis=-1, keepdims=True)
        acc_ref[...] = alpha * acc_ref[...] + _mm(p, vj)
        m_ref[...] = m_new

    def body(j, carry):
        k0 = pl.multiple_of(j * tile, tile)
        s = _mm_nt(q, k_ref[pl.ds(k0, tile), :]) * MLA_SCALE
        update(s, v_ref[pl.ds(k0, tile), :])
        return carry

    lax.fori_loop(0, i, body, 0)

    k0 = pl.multiple_of(i * tile, tile)
    s = _mm_nt(q, k_ref[pl.ds(k0, tile), :]) * MLA_SCALE
    qc = lax.broadcasted_iota(jnp.int32, (tile, tile), 0) // CHUNK
    kc = lax.broadcasted_iota(jnp.int32, (tile, tile), 1) // CHUNK
    update(jnp.where(kc <= qc, s, -jnp.inf), v_ref[pl.ds(k0, tile), :])
    o_ref[...] = (acc_ref[...] / l_ref[...]).astype(o_ref.dtype)


def _flash_attention(q, k, v, *, n_seq, seq_len, tile):
    heads = MLA_HEADS
    nq = seq_len // tile
    blocks = 2 * (tile * QK_PAD * 2 + seq_len * QK_PAD * 2 + seq_len * V_DIM * 2 + tile * V_DIM * 2)
    scratch = tile * (2 + V_DIM) * 4
    temps = 4 * tile * tile * 4
    return pl.pallas_call(
        functools.partial(_flash_kernel, tile=tile),
        grid=(n_seq, heads, nq),
        in_specs=[
            pl.BlockSpec((tile, QK_PAD), lambda b, h, i: (b * nq + i, h)),
            pl.BlockSpec((seq_len, QK_PAD), lambda b, h, i: (b, h)),
            pl.BlockSpec((seq_len, V_DIM), lambda b, h, i: (b, h)),
        ],
        out_specs=pl.BlockSpec((tile, V_DIM), lambda b, h, i: (b * nq + i, h)),
        out_shape=jax.ShapeDtypeStruct((n_seq * seq_len, heads * V_DIM), BF16),
        scratch_shapes=[pltpu.VMEM((tile, 1), F32), pltpu.VMEM((tile, 1), F32), pltpu.VMEM((tile, V_DIM), F32)],
        compiler_params=pltpu.CompilerParams(dimension_semantics=("arbitrary", "arbitrary", "arbitrary"),
                                             vmem_limit_bytes=_vmem_limit(blocks + scratch + temps)),
        name="mla_flash",
    )(q, k, v)


def _decode_attn_kernel(q_ref, ckv_c_ref, kr_c_ref, ckv_n_ref, kr_n_ref, wuk_ref, wuv_ref, o_ref, *, past, seq):
    heads = MLA_HEADS
    qall = q_ref[...]
    qlat, qrope = [], []
    for h in range(heads):
        qn = qall[:, h * QK_PAD:h * QK_PAD + LANES]
        qlat.append(_mm_nt(qn, wuk_ref[h]).astype(BF16))
        qrope.append(qall[:, h * QK_PAD + LANES:(h + 1) * QK_PAD])
    qlat = jnp.concatenate(qlat, axis=0)
    qrope = jnp.concatenate(qrope, axis=0)
    n_rows = heads * seq
    n_keys = past + seq
    pad = (-n_keys) % LANES
    ckv_parts = [ckv_c_ref[...].astype(BF16), ckv_n_ref[...].astype(BF16)]
    kr_parts = [kr_c_ref[...].astype(BF16), kr_n_ref[...].astype(BF16)]
    if pad:
        ckv_parts.append(jnp.zeros((pad, KV_LORA), BF16))
        kr_parts.append(jnp.zeros((pad, ROPE_DIM), BF16))
    ckv_all = jnp.concatenate(ckv_parts, axis=0)
    kr_all = jnp.concatenate(kr_parts, axis=0)
    kr_all = jnp.concatenate([kr_all, jnp.zeros((n_keys + pad, LANES - ROPE_DIM), BF16)], axis=1)
    s = (_mm_nt(qlat, ckv_all) + _mm_nt(qrope, kr_all)) * MLA_SCALE
    q_pos = past + lax.broadcasted_iota(jnp.int32, (n_rows, n_keys + pad), 0) % seq
    k_pos = lax.broadcasted_iota(jnp.int32, (n_rows, n_keys + pad), 1)
    visible = ((k_pos // CHUNK) <= (q_pos // CHUNK)) & (k_pos < n_keys)
    s = jnp.where(visible, s, -jnp.inf)
    p = jnp.exp(s - jnp.max(s, axis=-1, keepdims=True))
    p = p / jnp.sum(p, axis=-1, keepdims=True)
    o_lat = _mm(p, ckv_all)
    for h in range(heads):
        o_ref[:, h * V_DIM:(h + 1) * V_DIM] = _mm(o_lat[h * seq:(h + 1) * seq], wuv_ref[h]).astype(o_ref.dtype)


def _decode_attention(q, ckv_cache, kr_cache, ckv, kr, wuk_h, wuv_h, *, n_seq, seq, row0):
    heads = MLA_HEADS
    past = ckv_cache.shape[1]
    blk0 = row0 // seq
    tok = lambda b: (blk0 + b, 0)
    fix3 = lambda b: (0, 0, 0)
    blocks = 2 * (seq * heads * QK_PAD * 2 + past * (KV_LORA + LANES) * 4 + seq * (KV_LORA + LANES) * 4
                  + seq * heads * V_DIM * 2) + 2 * 2 * heads * KV_LORA * LANES * 2
    temps = (past + seq + LANES) * (KV_LORA + LANES) * 2 * 2 + 6 * heads * seq * (past + seq + LANES) * 4
    return pl.pallas_call(
        functools.partial(_decode_attn_kernel, past=past, seq=seq),
        grid=(n_seq,),
        in_specs=[
            pl.BlockSpec((seq, heads * QK_PAD), tok),
            pl.BlockSpec((None, past, KV_LORA), lambda b: (b, 0, 0)),
            pl.BlockSpec((None, past, ROPE_DIM), lambda b: (b, 0, 0)),
            pl.BlockSpec((seq, KV_LORA), tok), pl.BlockSpec((seq, ROPE_DIM), tok),
            _resident((heads, KV_LORA, NOPE_DIM), fix3), _resident((heads, KV_LORA, V_DIM), fix3),
        ],
        out_specs=pl.BlockSpec((seq, heads * V_DIM), lambda b: (b, 0)),
        out_shape=jax.ShapeDtypeStruct((n_seq * seq, heads * V_DIM), BF16),
        compiler_params=pltpu.CompilerParams(dimension_semantics=("arbitrary",),
                                             vmem_limit_bytes=_vmem_limit(blocks + temps)),
        name="mla_decode",
    )(q, ckv_cache, kr_cache, ckv, kr, wuk_h, wuv_h)


def _largest_tile(limit, *sizes):
    t = limit
    while any(s % t for s in sizes):
        t //= 2
    return t


def _rope_tables(pos):
    half = ROPE_DIM // 2
    inv_freq = ROPE_THETA ** (-jnp.arange(half, dtype=F32) / half)
    ang = pos.astype(F32)[:, None] * inv_freq[None, :]
    cos, sin = jnp.cos(ang), jnp.sin(ang)
    zero = jnp.zeros_like(cos)
    return (jnp.concatenate([cos, cos, zero, zero], axis=1), jnp.concatenate([-sin, zero, zero, zero], axis=1),
            jnp.concatenate([zero, sin, zero, zero], axis=1))


def _pad_cols(w, width):
    return jnp.pad(w, ((0, 0), (0, width - w.shape[1])))


def kernel(x_prompt, x_sample, state_dn_conv, state_dn_recurrent, cache_mla_ckv, cache_mla_krope, p_prompt, p_sample, ln1_g, ln1_b, ln2_g, ln2_b, mlp_w_up, mlp_w_down, ple_w_proj, ple_norm, ple_w_gate, dn_w_in, dn_conv_w, dn_a_log, dn_dt_bias, dn_o_norm, dn_w_o, mla_w_in, mla_q_norm, mla_w_uq, mla_kv_norm, mla_w_uk, mla_w_uv, mla_w_o):
    bp, lp, d = x_prompt.shape
    bs, ls, _ = x_sample.shape
    depth = ln1_g.shape[0]
    past = cache_mla_ckv.shape[2]
    tp, ts = bp * lp, bs * ls
    alpha = (2 * depth) ** 0.25
    tm = _largest_tile(TOKEN_TILE, tp, ts)
    heads = DN_HEADS
    qk_dim = heads * DN_DK
    conv_dim = 2 * qk_dim + heads * DN_DV
    v_dim = heads * DN_DV

    y = jnp.concatenate([x_prompt.reshape(tp, d), x_sample.reshape(ts, d)], axis=0)
    row = lambda a: a.reshape(1, -1)

    pos = jnp.concatenate([jnp.tile(jnp.arange(lp), bp), jnp.tile(past + jnp.arange(ls), bs)])
    cos_t, sin_lo_t, sin_hi_t = _rope_tables(pos)

    p_conv, p_rec, p_ckv, p_kr = [], [], [], []
    s_conv, s_rec, s_ckv, s_kr = [], [], [], []
    for i in range(depth):
        j = i // N_MIXERS
        if i % N_MIXERS == 0:
            w_in = dn_w_in[j]
            wqkv = w_in[:, :conv_dim].astype(BF16)
            wz = w_in[:, conv_dim:conv_dim + v_dim].astype(BF16)
            wba = _pad_cols(w_in[:, conv_dim + v_dim:], LANES).astype(BF16)
            qkv, z, ba = _dn_proj(y, wqkv, wz, wba, tm=tm)
            cw = jnp.pad(dn_conv_w[j], ((0, SUBLANES - CONV_W), (0, 0)))
            alog = jnp.pad(dn_a_log[j], (heads, LANES - 2 * heads)).reshape(1, LANES)
            dtb = jnp.pad(dn_dt_bias[j], (heads, LANES - 2 * heads)).reshape(1, LANES)
            onorm = row(dn_o_norm[j])
            hist = lambda st: jnp.pad(st, ((0, 0), (SUBLANES - (CONV_W - 1), 0), (0, 0)))
            cp = min(CHUNK, lp)
            o_p, ct_p, st_p = _dn_core(
                qkv, z, ba, jnp.zeros((bp, SUBLANES, conv_dim), F32), jnp.zeros((bp, heads, DN_DK, DN_DV), F32),
                cw, alog, dtb, onorm, n_seq=bp, seq_len=lp, row0=0, tb=_largest_tile(DN_STEP_TOKENS, lp), c=cp)
            cs = min(CHUNK, ls)
            o_s, ct_s, st_s = _dn_core(
                qkv, z, ba, hist(state_dn_conv[j]), state_dn_recurrent[j],
                cw, alog, dtb, onorm, n_seq=bs, seq_len=ls, row0=tp, tb=ls, c=cs)
            p_conv.append(ct_p[:, SUBLANES - (CONV_W - 1):])
            p_rec.append(st_p)
            s_conv.append(ct_s[:, SUBLANES - (CONV_W - 1):])
            s_rec.append(st_s)
            w_o = dn_w_o[j]
        else:
            w_in = _pad_cols(mla_w_in[j], Q_LORA + KV_LORA + LANES).astype(BF16)
            w_uq = mla_w_uq[j].reshape(Q_LORA, MLA_HEADS, NOPE_DIM + ROPE_DIM)
            wqn = w_uq[:, :, :NOPE_DIM].reshape(Q_LORA, MLA_HEADS * NOPE_DIM).astype(BF16)
            wqr = jnp.pad(w_uq[:, :, NOPE_DIM:], ((0, 0), (0, 0), (0, LANES - ROPE_DIM))
                          ).reshape(Q_LORA, MLA_HEADS * LANES).astype(BF16)
            wuk = mla_w_uk[j].astype(BF16)
            wuv = mla_w_uv[j].astype(BF16)
            q, k, v, ckv, kr = _mla_proj(
                y, w_in, row(mla_q_norm[j]), row(mla_kv_norm[j]), wqn, wqr,
                wuk.reshape(KV_LORA, MLA_HEADS * NOPE_DIM), wuv.reshape(KV_LORA, MLA_HEADS * V_DIM),
                cos_t, sin_lo_t, sin_hi_t, tm=tm)
            o_p = _flash_attention(q, k, v, n_seq=bp, seq_len=lp, tile=_largest_tile(ATTN_TILE, lp))
            o_s = _decode_attention(q, cache_mla_ckv[j], cache_mla_krope[j], ckv, kr,
                                    jnp.transpose(wuk, (1, 0, 2)), jnp.transpose(wuv, (1, 0, 2)),
                                    n_seq=bs, seq=ls, row0=tp)
            p_ckv.append(ckv[:tp].reshape(bp, lp, KV_LORA))
            p_kr.append(kr[:tp].reshape(bp, lp, ROPE_DIM))
            s_ckv.append(ckv[tp:].reshape(bs, ls, KV_LORA))
            s_kr.append(kr[tp:].reshape(bs, ls, ROPE_DIM))
            w_o = mla_w_o[j]
        o = jnp.concatenate([o_p, o_s], axis=0)
        p = jnp.concatenate([p_prompt[i].reshape(tp, -1), p_sample[i].reshape(ts, -1)], axis=0)
        y = _finish_layer(y, o, p, w_o.astype(BF16), row(ln1_g[i]), row(ln1_b[i]), mlp_w_up[i].astype(BF16),
                          mlp_w_down[i].astype(BF16), row(ln2_g[i]), row(ln2_b[i]), ple_w_proj[i].astype(BF16),
                          row(ple_norm[i]), ple_w_gate[i].astype(BF16), alpha=alpha, tm=tm)
    return (y[:tp].reshape(bp, lp, d), y[tp:].reshape(bs, ls, d),
            jnp.stack(p_conv), jnp.stack(p_rec), jnp.stack(p_ckv), jnp.stack(p_kr),
            jnp.stack(s_conv), jnp.stack(s_rec), jnp.stack(s_ckv), jnp.stack(s_kr))
```

```python
import functools
import math

import jax
import jax.numpy as jnp
from jax import lax
from jax.experimental import pallas as pl
from jax.experimental.pallas import tpu as pltpu

F32 = jnp.float32
BF16 = jnp.bfloat16

CHUNK = 64
N_MIXERS = 2
EPS = 1e-6
DN_HEADS = 8
DN_DK = 128
DN_DV = 128
CONV_W = 4
MLA_HEADS = 8
Q_LORA = 512
KV_LORA = 256
NOPE_DIM = 128
ROPE_DIM = 64
V_DIM = 128
MLA_SCALE = (NOPE_DIM + ROPE_DIM) ** -0.5
ROPE_THETA = 10000.0
Q_PRESCALE = MLA_SCALE * math.log2(math.e)

LANES = 128
SUBLANES = 8
VMEM_BUDGET_BYTES = 60000 * 1024

TOKEN_TILE = 512
FF_CHUNK = 1024
DN_STEP_TOKENS = 256
ATTN_Q_TILE = 1024
ATTN_K_TILE = 512
QK_PAD = 2 * LANES


def _vmem_limit(nbytes):
    return int(min(VMEM_BUDGET_BYTES, nbytes))


def _nbytes(shape, dtype):
    return math.prod(shape) * jnp.dtype(dtype).itemsize


def _resident(shape, index_map):
    return pl.BlockSpec(shape, index_map, pipeline_mode=pl.Buffered(1))


def _mm(a, b):
    return jnp.dot(a.astype(BF16), b.astype(BF16), preferred_element_type=F32)


def _mm_nt(a, b):
    return lax.dot_general(a.astype(BF16), b.astype(BF16), (((1,), (1,)), ((), ())),
                           preferred_element_type=F32)


def _mm_tn(a, b):
    return pl.dot(a.astype(BF16), b.astype(BF16), trans_a=True)


def _split2(x):
    hi = x.astype(BF16)
    lo = (x - hi.astype(F32)).astype(BF16)
    return hi, lo


def _mm3(a, b):
    ah, al = _split2(a)
    bh, bl = _split2(b)
    return jnp.dot(jnp.concatenate([ah, al, ah], axis=1), jnp.concatenate([bh, bh, bl], axis=0),
                   preferred_element_type=F32)


def _mm_exact_lhs3(a_bf16, b):
    b0 = b.astype(BF16)
    r1 = b - b0.astype(F32)
    b1 = r1.astype(BF16)
    b2 = (r1 - b1.astype(F32)).astype(BF16)
    return _mm(a_bf16, b0) + (_mm(a_bf16, b1) + _mm(a_bf16, b2))


def _layer_norm(x, g, b):
    xc = x - jnp.mean(x, axis=-1, keepdims=True)
    var = jnp.mean(xc * xc, axis=-1, keepdims=True)
    return xc * lax.rsqrt(var + EPS) * g + b


def _rms_norm(x, g):
    return x * lax.rsqrt(jnp.mean(x * x, axis=-1, keepdims=True) + EPS) * g


def _sigmoid(x):
    return 1.0 / (1.0 + jnp.exp(-x))


def _softplus(x):
    return jnp.maximum(x, 0.0) + jnp.log1p(jnp.exp(-jnp.abs(x)))


def _finish_kernel(y_ref, o_ref, p_ref, wo_ref, ln1g_ref, ln1b_ref, wup_ref, wdn_ref, ln2g_ref, ln2b_ref,
                   wpp_ref, pn_ref, wg_ref, out_ref, *, alpha, d_ff):
    y = y_ref[...]
    m = _mm(o_ref[...], wo_ref[...])
    y1 = _layer_norm(alpha * y + m, ln1g_ref[...], ln1b_ref[...])
    y1b = y1.astype(BF16)
    acc = jnp.zeros_like(y1)
    for c in range(d_ff // FF_CHUNK):
        h = _mm(y1b, wup_ref[:, c * FF_CHUNK:(c + 1) * FF_CHUNK])
        h = jnp.square(jnp.maximum(h, 0.0))
        acc = acc + _mm(h, wdn_ref[c * FF_CHUNK:(c + 1) * FF_CHUNK, :])
    y2 = _layer_norm(alpha * y1 + acc, ln2g_ref[...], ln2b_ref[...])
    e = _rms_norm(_mm(p_ref[...], wpp_ref[...]), pn_ref[...])
    gate = _sigmoid(_mm(y2, wg_ref[...]))
    out_ref[...] = y2 + gate * e


def _finish_layer(y, o, p, wo, ln1g, ln1b, wup, wdn, ln2g, ln2b, wpp, pn, wg, *, alpha, tm):
    t, d = y.shape
    d_ff = wup.shape[1]
    ple = p.shape[1]
    row = lambda i: (i, 0)
    fix = lambda i: (0, 0)
    weights = 2 * (d * d * 2 + d * d_ff + d_ff * d + ple * d)
    tiles = 2 * (2 * tm * d * 4 + tm * d * 2 + tm * ple * 4)
    temps = 6 * tm * d * 4 + 2 * tm * FF_CHUNK * 4
    return pl.pallas_call(
        functools.partial(_finish_kernel, alpha=alpha, d_ff=d_ff),
        grid=(t // tm,),
        in_specs=[
            pl.BlockSpec((tm, d), row), pl.BlockSpec((tm, d), row), pl.BlockSpec((tm, ple), row),
            _resident((d, d), fix), _resident((1, d), fix), _resident((1, d), fix),
            _resident((d, d_ff), fix), _resident((d_ff, d), fix), _resident((1, d), fix), _resident((1, d), fix),
            _resident((ple, d), fix), _resident((1, d), fix), _resident((d, d), fix),
        ],
        out_specs=pl.BlockSpec((tm, d), row),
        out_shape=jax.ShapeDtypeStruct((t, d), F32),
        compiler_params=pltpu.CompilerParams(dimension_semantics=("arbitrary",),
                                             vmem_limit_bytes=_vmem_limit(weights + tiles + temps)),
        name="finish_layer",
    )(y, o, p, wo, ln1g, ln1b, wup, wdn, ln2g, ln2b, wpp, pn, wg)


def _dn_proj_kernel(y_ref, wqkv_ref, wz_ref, wba_ref, qkv_ref, z_ref, ba_ref):
    yb = y_ref[...].astype(BF16)
    qkv_ref[...] = _mm(yb, wqkv_ref[...])
    z_ref[...] = _mm(yb, wz_ref[...])
    ba_ref[...] = _mm(yb, wba_ref[...])


def _dn_proj(y, wqkv, wz, wba, *, tm):
    t, d = y.shape
    nq, nz, nb = wqkv.shape[1], wz.shape[1], wba.shape[1]
    row = lambda i: (i, 0)
    fix = lambda i: (0, 0)
    weights = 2 * d * (nq + nz + nb)
    tiles = 2 * 4 * tm * (d + nq + nz + nb)
    temps = 2 * tm * d + 4 * tm * nq
    return pl.pallas_call(
        _dn_proj_kernel,
        grid=(t // tm,),
        in_specs=[pl.BlockSpec((tm, d), row), _resident((d, nq), fix), _resident((d, nz), fix),
                  _resident((d, nb), fix)],
        out_specs=[pl.BlockSpec((tm, nq), row), pl.BlockSpec((tm, nz), row), pl.BlockSpec((tm, nb), row)],
        out_shape=[jax.ShapeDtypeStruct((t, nq), F32), jax.ShapeDtypeStruct((t, nz), F32),
                   jax.ShapeDtypeStruct((t, nb), F32)],
        compiler_params=pltpu.CompilerParams(dimension_semantics=("arbitrary",),
                                             vmem_limit_bytes=_vmem_limit(weights + tiles + temps)),
        name="dn_proj",
    )(y, wqkv, wz, wba)


def _dn_core_kernel(qkv_ref, z_ref, ba_ref, cprev_ref, s0_ref, cw_ref, alog_ref, dtb_ref, onorm_ref,
                    o_ref, ctail_ref, sout_ref,
                    xs_ref, q_ref, k_ref, v_ref, s_ref, *, tb, c, n_steps):
    heads = DN_HEADS
    dk = DN_DK
    qk_dim = heads * dk
    g_heads = LANES // c
    n_groups = heads // g_heads
    step = pl.program_id(1)

    @pl.when(step == 0)
    def _():
        xs_ref[0:SUBLANES, :] = cprev_ref[...]
        s_ref[...] = s0_ref[...]

    x = qkv_ref[...]
    xs_ref[SUBLANES:SUBLANES + tb, :] = x
    cw = cw_ref[...]
    conv = x * cw[CONV_W - 1:CONV_W, :]
    for k in range(1, CONV_W):
        conv = conv + xs_ref[SUBLANES - k:SUBLANES - k + tb, :] * cw[CONV_W - 1 - k:CONV_W - k, :]
    tail = xs_ref[tb:tb + SUBLANES, :]
    xs_ref[0:SUBLANES, :] = tail

    @pl.when(step == n_steps - 1)
    def _():
        ctail_ref[...] = tail

    act = conv * _sigmoid(conv)
    for h in range(heads):
        qh = act[:, h * dk:(h + 1) * dk]
        kh = act[:, qk_dim + h * dk:qk_dim + (h + 1) * dk]
        q_ref[:, h * dk:(h + 1) * dk] = qh * lax.rsqrt(jnp.sum(qh * qh, axis=-1, keepdims=True) + EPS) * (dk ** -0.5)
        k_ref[:, h * dk:(h + 1) * dk] = kh * lax.rsqrt(jnp.sum(kh * kh, axis=-1, keepdims=True) + EPS)
    v_ref[...] = act[:, 2 * qk_dim:]

    rows = lax.broadcasted_iota(jnp.int32, (LANES, LANES), 0)
    cols = lax.broadcasted_iota(jnp.int32, (LANES, LANES), 1)
    same_block = (rows // c) == (cols // c)
    causal = same_block & (rows >= cols)
    strict = same_block & (rows > cols)
    eye = (rows == cols).astype(F32)
    lr = lax.broadcasted_iota(jnp.int32, (LANES, c), 0)
    lc = lax.broadcasted_iota(jnp.int32, (LANES, c), 1)
    cum_op = ((lc <= lr) & (lr < c)).astype(BF16)
    neg_exp_alog = -jnp.exp(alog_ref[...])
    dtb = dtb_ref[...]
    onorm = onorm_ref[...]

    def stack(pieces):
        return pieces[0] if len(pieces) == 1 else jnp.concatenate(pieces, axis=0)

    groups = [[p * g_heads + s for s in range(g_heads)] for p in range(n_groups)]

    def chunk_body(ci, carry):
        r0 = pl.multiple_of(ci * c, c)
        ba = ba_ref[pl.ds(r0, c), :]
        qp = [stack([q_ref[pl.ds(r0, c), h * dk:(h + 1) * dk] for h in hs]) for hs in groups]
        kp = [stack([k_ref[pl.ds(r0, c), h * dk:(h + 1) * dk] for h in hs]) for hs in groups]
        vp = [stack([v_ref[pl.ds(r0, c), h * dk:(h + 1) * dk] for h in hs]) for hs in groups]
        zp = [stack([z_ref[pl.ds(r0, c), h * DN_DV:(h + 1) * DN_DV] for h in hs]) for hs in groups]
        state = [s_ref[h] for h in range(heads)]

        beta_all = _sigmoid(ba)
        g_all = neg_exp_alog * _softplus(ba + dtb)
        gc = _mm_exact_lhs3(cum_op, g_all)
        gct = gc.T
        exp_gc = jnp.exp(gc)

        def col(x, r_lo, r_hi, lane):
            return jnp.broadcast_to(x[r_lo:r_hi, lane:lane + 1], (c, LANES))

        col_gc = [stack([col(gc, 0, c, heads + h) for h in hs]) for hs in groups]
        col_eg = [stack([col(exp_gc, 0, c, heads + h) for h in hs]) for hs in groups]
        col_gl = [stack([col(gc, c - 1, c, heads + h) for h in hs]) for hs in groups]
        col_beta = [stack([col(beta_all, 0, c, h) for h in hs]) for hs in groups]
        decay = []
        for p, hs in enumerate(groups):
            row_gc = gct[heads + hs[0]:heads + hs[0] + 1, :]
            for s in range(1, g_heads):
                row_gc = row_gc + pltpu.roll(gct[heads + hs[s]:heads + hs[s] + 1, :], s * c, 1)
            decay.append(jnp.exp(jnp.where(causal, col_gc[p] - row_gc, -jnp.inf)))
        kb = [kp[p] * col_beta[p] for p in range(n_groups)]
        a_low = [jnp.where(strict, _mm_nt(kb[p], kp[p]) * decay[p], 0.0) for p in range(n_groups)]
        attn = [_mm_nt(qp[p], kp[p]) * decay[p] for p in range(n_groups)]
        t_inv = [eye - a for a in a_low]
        pw = a_low
        for _ in range(int(math.log2(c)) - 1):
            pw = [_mm3(x, x) for x in pw]
            t_inv = [t + _mm3(t, x) for t, x in zip(t_inv, pw)]
        uw = [_mm3(t_inv[p], jnp.concatenate([vp[p] * col_beta[p], kb[p] * col_eg[p]], axis=1))
              for p in range(n_groups)]
        qg = [qp[p] * col_eg[p] for p in range(n_groups)]
        kd = [kp[p] * jnp.exp(col_gl[p] - col_gc[p]) for p in range(n_groups)]
        new_state, outs = [None] * heads, []
        for p, hs in enumerate(groups):
            u, w = uw[p][:, :DN_DV], uw[p][:, DN_DV:]
            ws, qs = [], []
            for s, h in enumerate(hs):
                wq = _mm(jnp.concatenate([w[s * c:(s + 1) * c], qg[p][s * c:(s + 1) * c]], axis=0), state[h])
                ws.append(wq[:c])
                qs.append(wq[c:])
            v_new = u - stack(ws)
            o = stack(qs) + _mm(attn[p], v_new)
            for s, h in enumerate(hs):
                kd_s = jnp.where((rows // c) == s, kd[p], 0.0)
                decay_last = jnp.exp(gc[c - 1:c, heads + h:heads + h + 1])
                new_state[h] = state[h] * decay_last + _mm_tn(kd_s, v_new)
            o = o * lax.rsqrt(jnp.mean(o * o, axis=-1, keepdims=True) + EPS) * onorm
            outs.append(o * (zp[p] * _sigmoid(zp[p])))

        for h in range(heads):
            s_ref[h] = new_state[h]
        for p, hs in enumerate(groups):
            for s, h in enumerate(hs):
                o_ref[pl.ds(r0, c), h * DN_DV:(h + 1) * DN_DV] = outs[p][s * c:(s + 1) * c].astype(o_ref.dtype)
        return carry

    lax.fori_loop(0, tb // c, chunk_body, 0)

    @pl.when(step == n_steps - 1)
    def _():
        sout_ref[...] = s_ref[...]


def _dn_core(qkv, z, ba, conv_prev, s0, conv_w, alog, dtb, onorm, *, n_seq, seq_len, row0, tb, c):
    heads = DN_HEADS
    n_steps = seq_len // tb
    blk0 = row0 // tb
    cdim = qkv.shape[1]
    vdim = z.shape[1]
    tok = lambda b, l: (blk0 + b * n_steps + l, 0)
    seq3 = lambda b, l: (b, 0, 0)
    seq4 = lambda b, l: (b, 0, 0, 0)
    fix = lambda b, l: (0, 0)
    tiles = 2 * tb * (4 * cdim + 4 * vdim + 4 * LANES + 2 * vdim)
    state = 5 * heads * DN_DK * DN_DV * 4 + 4 * SUBLANES * cdim * 4
    scratch = (tb + SUBLANES) * cdim * 4 + 3 * tb * vdim * 4
    temps = 3 * tb * cdim * 4
    return pl.pallas_call(
        functools.partial(_dn_core_kernel, tb=tb, c=c, n_steps=n_steps),
        grid=(n_seq, n_steps),
        in_specs=[
            pl.BlockSpec((tb, cdim), tok), pl.BlockSpec((tb, vdim), tok), pl.BlockSpec((tb, LANES), tok),
            pl.BlockSpec((None, SUBLANES, cdim), seq3), pl.BlockSpec((None, heads, DN_DK, DN_DV), seq4),
            _resident((SUBLANES, cdim), fix), _resident((1, LANES), fix), _resident((1, LANES), fix),
            _resident((1, LANES), fix),
        ],
        out_specs=[
            pl.BlockSpec((tb, vdim), lambda b, l: (b * n_steps + l, 0)),
            pl.BlockSpec((None, SUBLANES, cdim), seq3),
            pl.BlockSpec((None, heads, DN_DK, DN_DV), seq4),
        ],
        out_shape=[
            jax.ShapeDtypeStruct((n_seq * seq_len, vdim), BF16),
            jax.ShapeDtypeStruct((n_seq, SUBLANES, cdim), F32),
            jax.ShapeDtypeStruct((n_seq, heads, DN_DK, DN_DV), F32),
        ],
        scratch_shapes=[
            pltpu.VMEM((tb + SUBLANES, cdim), F32),
            pltpu.VMEM((tb, vdim), F32), pltpu.VMEM((tb, vdim), F32), pltpu.VMEM((tb, vdim), F32),
            pltpu.VMEM((heads, DN_DK, DN_DV), F32),
        ],
        compiler_params=pltpu.CompilerParams(dimension_semantics=("arbitrary", "arbitrary"),
                                             vmem_limit_bytes=_vmem_limit(tiles + state + scratch + temps)),
        name="dn_core",
    )(qkv, z, ba, conv_prev, s0, conv_w, alog, dtb, onorm)


def _rope128(x, cos, sin_lo, sin_hi):
    half = ROPE_DIM // 2
    return x * cos + pltpu.roll(x, LANES - half, 1) * sin_lo + pltpu.roll(x, half, 1) * sin_hi


def _mla_proj_kernel(y_ref, win_ref, qn_ref, kvn_ref, wqn_ref, wqr_ref, wuk_ref, wuv_ref,
                     cos_ref, sl_ref, sh_ref, q_ref, k_ref, v_ref, ckv_ref, kr_ref):
    heads = MLA_HEADS
    proj = _mm(y_ref[...], win_ref[...])
    cq = _rms_norm(proj[:, :Q_LORA], qn_ref[...]).astype(BF16)
    ckv = _rms_norm(proj[:, Q_LORA:Q_LORA + KV_LORA], kvn_ref[...])
    cos, sl, sh = cos_ref[...], sl_ref[...], sh_ref[...]
    kr = _rope128(proj[:, Q_LORA + KV_LORA:], cos, sl, sh)
    ckv_ref[...] = ckv
    kr_ref[...] = kr[:, :ROPE_DIM]
    ckvb = ckv.astype(BF16)
    qn = _mm(cq, wqn_ref[...])
    qr = _mm(cq, wqr_ref[...])
    kn = _mm(ckvb, wuk_ref[...])
    v_ref[...] = _mm(ckvb, wuv_ref[...]).astype(BF16)
    krb = kr.astype(BF16)
    for h in range(heads):
        lo = h * QK_PAD
        q_ref[:, lo:lo + LANES] = (qn[:, h * LANES:(h + 1) * LANES] * Q_PRESCALE).astype(BF16)
        q_ref[:, lo + LANES:lo + QK_PAD] = (
            _rope128(qr[:, h * LANES:(h + 1) * LANES], cos, sl, sh) * Q_PRESCALE).astype(BF16)
        k_ref[:, lo:lo + LANES] = kn[:, h * LANES:(h + 1) * LANES].astype(BF16)
        k_ref[:, lo + LANES:lo + QK_PAD] = krb


def _mla_proj(y, win, qn, kvn, wqn, wqr, wuk, wuv, cos, sl, sh, *, tm):
    t, d = y.shape
    heads = MLA_HEADS
    nin = win.shape[1]
    row = lambda i: (i, 0)
    fix = lambda i: (0, 0)
    weights = 2 * (d * nin + Q_LORA * 2 * heads * LANES + KV_LORA * 2 * heads * LANES)
    tiles = 2 * tm * (4 * d + 3 * 4 * LANES + 2 * 2 * heads * QK_PAD + 2 * heads * V_DIM + 4 * KV_LORA + 4 * ROPE_DIM)
    temps = tm * (4 * nin + 4 * 4 * heads * LANES + 2 * d)
    return pl.pallas_call(
        _mla_proj_kernel,
        grid=(t // tm,),
        in_specs=[
            pl.BlockSpec((tm, d), row), _resident((d, nin), fix), _resident((1, Q_LORA), fix),
            _resident((1, KV_LORA), fix), _resident((Q_LORA, heads * LANES), fix),
            _resident((Q_LORA, heads * LANES), fix), _resident((KV_LORA, heads * NOPE_DIM), fix),
            _resident((KV_LORA, heads * V_DIM), fix),
            pl.BlockSpec((tm, LANES), row), pl.BlockSpec((tm, LANES), row), pl.BlockSpec((tm, LANES), row),
        ],
        out_specs=[
            pl.BlockSpec((tm, heads * QK_PAD), row), pl.BlockSpec((tm, heads * QK_PAD), row),
            pl.BlockSpec((tm, heads * V_DIM), row), pl.BlockSpec((tm, KV_LORA), row),
            pl.BlockSpec((tm, ROPE_DIM), row),
        ],
        out_shape=[
            jax.ShapeDtypeStruct((t, heads * QK_PAD), BF16), jax.ShapeDtypeStruct((t, heads * QK_PAD), BF16),
            jax.ShapeDtypeStruct((t, heads * V_DIM), BF16), jax.ShapeDtypeStruct((t, KV_LORA), F32),
            jax.ShapeDtypeStruct((t, ROPE_DIM), F32),
        ],
        compiler_params=pltpu.CompilerParams(dimension_semantics=("arbitrary",),
                                             vmem_limit_bytes=_vmem_limit(weights + tiles + temps)),
        name="mla_proj",
    )(y, win, qn, kvn, wqn, wqr, wuk, wuv, cos, sl, sh)


def _flash_kernel(q_ref, k_ref, v_ref, o_ref, m_ref, l_ref, acc_ref, *, tq, tk):
    i = pl.program_id(2)
    n_sub = tq // tk
    reps = tk // LANES

    def online_softmax(stats, q, kj, vj, mask):
        m_prev, l_prev, acc = stats
        s = _mm_nt(q, kj)
        if mask is not None:
            s = jnp.where(mask, s, -jnp.inf)
        m_new = jnp.maximum(m_prev, jnp.max(s, axis=-1, keepdims=True))
        alpha = jnp.exp2(m_prev - m_new)
        p = jnp.exp2(s - jnp.concatenate([m_new] * reps, axis=1))
        l_new = alpha * l_prev + jnp.sum(p, axis=-1, keepdims=True)
        return m_new, l_new, alpha * acc + _mm(p, vj)

    m_ref[...] = jnp.full(m_ref.shape, -jnp.inf, F32)
    l_ref[...] = jnp.zeros(l_ref.shape, F32)
    acc_ref[...] = jnp.zeros(acc_ref.shape, F32)

    def body(j, carry):
        q = q_ref[...]
        tiles = []
        for u in range(n_sub):
            k0 = pl.multiple_of((j * n_sub + u) * tk, tk)
            tiles.append((k_ref[pl.ds(k0, tk), :], v_ref[pl.ds(k0, tk), :]))
        stats = (m_ref[...], l_ref[...], acc_ref[...])
        for kj, vj in tiles:
            stats = online_softmax(stats, q, kj, vj, None)
        m_ref[...], l_ref[...], acc_ref[...] = stats
        return carry

    lax.fori_loop(0, i, body, 0)

    for u in range(n_sub):
        k0 = pl.multiple_of(i * tq + u * tk, tk)
        rows = slice(u * tk, tq)
        nr = tq - u * tk
        qc = (lax.broadcasted_iota(jnp.int32, (nr, tk), 0) + u * tk) // CHUNK
        kc = (lax.broadcasted_iota(jnp.int32, (nr, tk), 1) + u * tk) // CHUNK
        stats = (m_ref[rows, :], l_ref[rows, :], acc_ref[rows, :])
        stats = online_softmax(stats, q_ref[rows, :], k_ref[pl.ds(k0, tk), :], v_ref[pl.ds(k0, tk), :], kc <= qc)
        m_ref[rows, :], l_ref[rows, :], acc_ref[rows, :] = stats
    o_ref[...] = (acc_ref[...] / l_ref[...]).astype(o_ref.dtype)


def _flash_attention(q, k, v, *, n_seq, seq_len, tq, tk):
    heads = MLA_HEADS
    nq = seq_len // tq
    blocks = 2 * (tq * QK_PAD * 2 + seq_len * QK_PAD * 2 + seq_len * V_DIM * 2 + tq * V_DIM * 2)
    scratch = 3 * tq * LANES * 4
    temps = (tq // tk + 2) * tq * tk * 4
    return pl.pallas_call(
        functools.partial(_flash_kernel, tq=tq, tk=tk),
        grid=(n_seq, heads, nq),
        in_specs=[
            pl.BlockSpec((tq, QK_PAD), lambda b, h, i: (b * nq + i, h)),
            pl.BlockSpec((seq_len, QK_PAD), lambda b, h, i: (b, h)),
            pl.BlockSpec((seq_len, V_DIM), lambda b, h, i: (b, h)),
        ],
        out_specs=pl.BlockSpec((tq, V_DIM), lambda b, h, i: (b * nq + i, h)),
        out_shape=jax.ShapeDtypeStruct((n_seq * seq_len, heads * V_DIM), BF16),
        scratch_shapes=[pltpu.VMEM((tq, LANES), F32), pltpu.VMEM((tq, LANES), F32), pltpu.VMEM((tq, V_DIM), F32)],
        compiler_params=pltpu.CompilerParams(dimension_semantics=("arbitrary", "arbitrary", "arbitrary"),
                                             vmem_limit_bytes=_vmem_limit(blocks + scratch + temps)),
        name="mla_flash",
    )(q, k, v)


def _decode_attn_kernel(q_ref, ckv_c_ref, kr_c_ref, ckv_n_ref, kr_n_ref, wuk_ref, wuv_ref, o_ref, *, past, seq):
    heads = MLA_HEADS
    qall = q_ref[...]
    qlat, qrope = [], []
    for h in range(heads):
        qn = qall[:, h * QK_PAD:h * QK_PAD + LANES]
        qlat.append(_mm_nt(qn, wuk_ref[h]).astype(BF16))
        qrope.append(qall[:, h * QK_PAD + LANES:(h + 1) * QK_PAD])
    qlat = jnp.concatenate(qlat, axis=0)
    qrope = jnp.concatenate(qrope, axis=0)
    n_rows = heads * seq
    n_keys = past + seq
    pad = (-n_keys) % LANES
    ckv_parts = [ckv_c_ref[...].astype(BF16), ckv_n_ref[...].astype(BF16)]
    kr_parts = [kr_c_ref[...].astype(BF16), kr_n_ref[...].astype(BF16)]
    if pad:
        ckv_parts.append(jnp.zeros((pad, KV_LORA), BF16))
        kr_parts.append(jnp.zeros((pad, ROPE_DIM), BF16))
    ckv_all = jnp.concatenate(ckv_parts, axis=0)
    kr_all = jnp.concatenate(kr_parts, axis=0)
    kr_all = jnp.concatenate([kr_all, jnp.zeros((n_keys + pad, LANES - ROPE_DIM), BF16)], axis=1)
    s = _mm_nt(qlat, ckv_all) + _mm_nt(qrope, kr_all)
    q_pos = past + lax.broadcasted_iota(jnp.int32, (n_rows, n_keys + pad), 0) % seq
    k_pos = lax.broadcasted_iota(jnp.int32, (n_rows, n_keys + pad), 1)
    visible = ((k_pos // CHUNK) <= (q_pos // CHUNK)) & (k_pos < n_keys)
    s = jnp.where(visible, s, -jnp.inf)
    p = jnp.exp2(s - jnp.max(s, axis=-1, keepdims=True))
    p = p / jnp.sum(p, axis=-1, keepdims=True)
    o_lat = _mm(p, ckv_all)
    for h in range(heads):
        o_ref[:, h * V_DIM:(h + 1) * V_DIM] = _mm(o_lat[h * seq:(h + 1) * seq], wuv_ref[h]).astype(o_ref.dtype)


def _decode_attention(q, ckv_cache, kr_cache, ckv, kr, wuk_h, wuv_h, *, n_seq, seq, row0):
    heads = MLA_HEADS
    past = ckv_cache.shape[1]
    blk0 = row0 // seq
    tok = lambda b: (blk0 + b, 0)
    fix3 = lambda b: (0, 0, 0)
    blocks = 2 * (seq * heads * QK_PAD * 2 + past * (KV_LORA + LANES) * 4 + seq * (KV_LORA + LANES) * 4
                  + seq * heads * V_DIM * 2) + 2 * 2 * heads * KV_LORA * LANES * 2
    temps = (past + seq + LANES) * (KV_LORA + LANES) * 2 * 2 + 6 * heads * seq * (past + seq + LANES) * 4
    return pl.pallas_call(
        functools.partial(_decode_attn_kernel, past=past, seq=seq),
        grid=(n_seq,),
        in_specs=[
            pl.BlockSpec((seq, heads * QK_PAD), tok),
            pl.BlockSpec((None, past, KV_LORA), lambda b: (b, 0, 0)),
            pl.BlockSpec((None, past, ROPE_DIM), lambda b: (b, 0, 0)),
            pl.BlockSpec((seq, KV_LORA), tok), pl.BlockSpec((seq, ROPE_DIM), tok),
            _resident((heads, KV_LORA, NOPE_DIM), fix3), _resident((heads, KV_LORA, V_DIM), fix3),
        ],
        out_specs=pl.BlockSpec((seq, heads * V_DIM), lambda b: (b, 0)),
        out_shape=jax.ShapeDtypeStruct((n_seq * seq, heads * V_DIM), BF16),
        compiler_params=pltpu.CompilerParams(dimension_semantics=("arbitrary",),
                                             vmem_limit_bytes=_vmem_limit(blocks + temps)),
        name="mla_decode",
    )(q, ckv_cache, kr_cache, ckv, kr, wuk_h, wuv_h)


def _largest_tile(limit, *sizes):
    t = limit
    while any(s % t for s in sizes):
        t //= 2
    return t


def _rope_tables(pos):
    half = ROPE_DIM // 2
    inv_freq = ROPE_THETA ** (-jnp.arange(half, dtype=F32) / half)
    ang = pos.astype(F32)[:, None] * inv_freq[None, :]
    cos, sin = jnp.cos(ang), jnp.sin(ang)
    zero = jnp.zeros_like(cos)
    return (jnp.concatenate([cos, cos, zero, zero], axis=1), jnp.concatenate([-sin, zero, zero, zero], axis=1),
            jnp.concatenate([zero, sin, zero, zero], axis=1))


def _pad_cols(w, width):
    return jnp.pad(w, ((0, 0), (0, width - w.shape[1])))


def kernel(x_prompt, x_sample, state_dn_conv, state_dn_recurrent, cache_mla_ckv, cache_mla_krope, p_prompt, p_sample, ln1_g, ln1_b, ln2_g, ln2_b, mlp_w_up, mlp_w_down, ple_w_proj, ple_norm, ple_w_gate, dn_w_in, dn_conv_w, dn_a_log, dn_dt_bias, dn_o_norm, dn_w_o, mla_w_in, mla_q_norm, mla_w_uq, mla_kv_norm, mla_w_uk, mla_w_uv, mla_w_o):
    bp, lp, d = x_prompt.shape
    bs, ls, _ = x_sample.shape
    depth = ln1_g.shape[0]
    past = cache_mla_ckv.shape[2]
    tp, ts = bp * lp, bs * ls
    alpha = (2 * depth) ** 0.25
    tm = _largest_tile(TOKEN_TILE, tp, ts)
    heads = DN_HEADS
    qk_dim = heads * DN_DK
    conv_dim = 2 * qk_dim + heads * DN_DV
    v_dim = heads * DN_DV

    y = jnp.concatenate([x_prompt.reshape(tp, d), x_sample.reshape(ts, d)], axis=0)
    row = lambda a: a.reshape(1, -1)

    pos = jnp.concatenate([jnp.tile(jnp.arange(lp), bp), jnp.tile(past + jnp.arange(ls), bs)])
    cos_t, sin_lo_t, sin_hi_t = _rope_tables(pos)

    p_conv, p_rec, p_ckv, p_kr = [], [], [], []
    s_conv, s_rec, s_ckv, s_kr = [], [], [], []
    for i in range(depth):
        j = i // N_MIXERS
        if i % N_MIXERS == 0:
            w_in = dn_w_in[j]
            wqkv = w_in[:, :conv_dim].astype(BF16)
            wz = w_in[:, conv_dim:conv_dim + v_dim].astype(BF16)
            wba = _pad_cols(w_in[:, conv_dim + v_dim:], LANES).astype(BF16)
            qkv, z, ba = _dn_proj(y, wqkv, wz, wba, tm=tm)
            cw = jnp.pad(dn_conv_w[j], ((0, SUBLANES - CONV_W), (0, 0)))
            alog = jnp.pad(dn_a_log[j], (heads, LANES - 2 * heads)).reshape(1, LANES)
            dtb = jnp.pad(dn_dt_bias[j], (heads, LANES - 2 * heads)).reshape(1, LANES)
            onorm = row(dn_o_norm[j])
            hist = lambda st: jnp.pad(st, ((0, 0), (SUBLANES - (CONV_W - 1), 0), (0, 0)))
            cp = min(CHUNK, lp)
            o_p, ct_p, st_p = _dn_core(
                qkv, z, ba, jnp.zeros((bp, SUBLANES, conv_dim), F32), jnp.zeros((bp, heads, DN_DK, DN_DV), F32),
                cw, alog, dtb, onorm, n_seq=bp, seq_len=lp, row0=0, tb=_largest_tile(DN_STEP_TOKENS, lp), c=cp)
            cs = min(CHUNK, ls)
            o_s, ct_s, st_s = _dn_core(
                qkv, z, ba, hist(state_dn_conv[j]), state_dn_recurrent[j],
                cw, alog, dtb, onorm, n_seq=bs, seq_len=ls, row0=tp, tb=ls, c=cs)
            p_conv.append(ct_p[:, SUBLANES - (CONV_W - 1):])
            p_rec.append(st_p)
            s_conv.append(ct_s[:, SUBLANES - (CONV_W - 1):])
            s_rec.append(st_s)
            w_o = dn_w_o[j]
        else:
            w_in = _pad_cols(mla_w_in[j], Q_LORA + KV_LORA + LANES).astype(BF16)
            w_uq = mla_w_uq[j].reshape(Q_LORA, MLA_HEADS, NOPE_DIM + ROPE_DIM)
            wqn = w_uq[:, :, :NOPE_DIM].reshape(Q_LORA, MLA_HEADS * NOPE_DIM).astype(BF16)
            wqr = jnp.pad(w_uq[:, :, NOPE_DIM:], ((0, 0), (0, 0), (0, LANES - ROPE_DIM))
                          ).reshape(Q_LORA, MLA_HEADS * LANES).astype(BF16)
            wuk = mla_w_uk[j].astype(BF16)
            wuv = mla_w_uv[j].astype(BF16)
            q, k, v, ckv, kr = _mla_proj(
                y, w_in, row(mla_q_norm[j]), row(mla_kv_norm[j]), wqn, wqr,
                wuk.reshape(KV_LORA, MLA_HEADS * NOPE_DIM), wuv.reshape(KV_LORA, MLA_HEADS * V_DIM),
                cos_t, sin_lo_t, sin_hi_t, tm=tm)
            tq = _largest_tile(ATTN_Q_TILE, lp)
            o_p = _flash_attention(q, k, v, n_seq=bp, seq_len=lp, tq=tq, tk=min(ATTN_K_TILE, tq))
            o_s = _decode_attention(q, cache_mla_ckv[j], cache_mla_krope[j], ckv, kr,
                                    jnp.transpose(wuk, (1, 0, 2)), jnp.transpose(wuv, (1, 0, 2)),
                                    n_seq=bs, seq=ls, row0=tp)
            p_ckv.append(ckv[:tp].reshape(bp, lp, KV_LORA))
            p_kr.append(kr[:tp].reshape(bp, lp, ROPE_DIM))
            s_ckv.append(ckv[tp:].reshape(bs, ls, KV_LORA))
            s_kr.append(kr[tp:].reshape(bs, ls, ROPE_DIM))
            w_o = mla_w_o[j]
        o = jnp.concatenate([o_p, o_s], axis=0)
        p = jnp.concatenate([p_prompt[i].reshape(tp, -1), p_sample[i].reshape(ts, -1)], axis=0)
        y = _finish_layer(y, o, p, w_o.astype(BF16), row(ln1_g[i]), row(ln1_b[i]), mlp_w_up[i].astype(BF16),
                          mlp_w_down[i].astype(BF16), row(ln2_g[i]), row(ln2_b[i]), ple_w_proj[i].astype(BF16),
                          row(ple_norm[i]), ple_w_gate[i].astype(BF16), alpha=alpha, tm=tm)
    return (y[:tp].reshape(bp, lp, d), y[tp:].reshape(bs, ls, d),
            jnp.stack(p_conv), jnp.stack(p_rec), jnp.stack(p_ckv), jnp.stack(p_kr),
            jnp.stack(s_conv), jnp.stack(s_rec), jnp.stack(s_ckv), jnp.stack(s_kr))
```

```python
import functools
import math

import jax
import jax.numpy as jnp
from jax import lax
from jax.experimental import pallas as pl
from jax.experimental.pallas import tpu as pltpu

F32 = jnp.float32
BF16 = jnp.bfloat16

CHUNK = 64
N_MIXERS = 2
EPS = 1e-6
DN_HEADS = 8
DN_DK = 128
DN_DV = 128
CONV_W = 4
MLA_HEADS = 8
Q_LORA = 512
KV_LORA = 256
NOPE_DIM = 128
ROPE_DIM = 64
V_DIM = 128
MLA_SCALE = (NOPE_DIM + ROPE_DIM) ** -0.5
ROPE_THETA = 10000.0
Q_PRESCALE = MLA_SCALE * math.log2(math.e)

LANES = 128
SUBLANES = 8
VMEM_BUDGET_BYTES = 60000 * 1024

TOKEN_TILE = 512
FF_CHUNK = 1024
DN_STEP_TOKENS = 256
DN_CHUNKS_PER_ITER = 4
ATTN_Q_TILE = 2048
ATTN_K_TILE = 512
QK_PAD = 2 * LANES


def _vmem_limit(nbytes):
    return int(min(VMEM_BUDGET_BYTES, nbytes))


def _nbytes(shape, dtype):
    return math.prod(shape) * jnp.dtype(dtype).itemsize


def _resident(shape, index_map):
    return pl.BlockSpec(shape, index_map, pipeline_mode=pl.Buffered(1))


def _mm(a, b):
    return jnp.dot(a.astype(BF16), b.astype(BF16), preferred_element_type=F32)


def _mm_nt(a, b):
    return lax.dot_general(a.astype(BF16), b.astype(BF16), (((1,), (1,)), ((), ())),
                           preferred_element_type=F32)


def _mm_tn(a, b):
    return pl.dot(a.astype(BF16), b.astype(BF16), trans_a=True)


def _split2(x):
    hi = x.astype(BF16)
    lo = (x - hi.astype(F32)).astype(BF16)
    return hi, lo


def _mm3(a, b):
    ah, al = _split2(a)
    bh, bl = _split2(b)
    n = b.shape[1]
    rhs = jnp.concatenate([jnp.concatenate([bh, bl], axis=1),
                           jnp.concatenate([bh, jnp.zeros_like(bl)], axis=1)], axis=0)
    r = jnp.dot(jnp.concatenate([ah, al], axis=1), rhs, preferred_element_type=F32)
    return r[:, :n] + r[:, n:]


def _mm_exact_lhs3(a_bf16, b):
    b0 = b.astype(BF16)
    r1 = b - b0.astype(F32)
    b1 = r1.astype(BF16)
    b2 = (r1 - b1.astype(F32)).astype(BF16)
    return _mm(a_bf16, b0) + (_mm(a_bf16, b1) + _mm(a_bf16, b2))


def _layer_norm(x, g, b):
    xc = x - jnp.mean(x, axis=-1, keepdims=True)
    var = jnp.mean(xc * xc, axis=-1, keepdims=True)
    return xc * lax.rsqrt(var + EPS) * g + b


def _rms_norm(x, g):
    return x * lax.rsqrt(jnp.mean(x * x, axis=-1, keepdims=True) + EPS) * g


def _sigmoid(x):
    return 1.0 / (1.0 + jnp.exp(-x))


def _softplus(x):
    return jnp.maximum(x, 0.0) + jnp.log1p(jnp.exp(-jnp.abs(x)))


def _stream_specs(tm, width, n_prompt_tiles, first_tile=0):
    prompt = pl.BlockSpec((tm, width), lambda i: (first_tile + jnp.minimum(i, n_prompt_tiles - 1), 0))
    sample = pl.BlockSpec((tm, width), lambda i: (jnp.maximum(i - n_prompt_tiles, 0), 0))
    return prompt, sample


def _read_stream(on_prompt, prompt_ref, sample_ref):
    return jnp.where(on_prompt, prompt_ref[...], sample_ref[...])


def _write_stream(on_prompt, prompt_ref, sample_ref, value):
    @pl.when(on_prompt)
    def _():
        prompt_ref[...] = value.astype(prompt_ref.dtype)

    @pl.when(jnp.logical_not(on_prompt))
    def _():
        sample_ref[...] = value.astype(sample_ref.dtype)


def _finish_kernel(yp_ref, ys_ref, op_ref, os_ref, pp_ref, ps_ref, wo_ref, ln1g_ref, ln1b_ref, wup_ref, wdn_ref,
                   ln2g_ref, ln2b_ref, wpp_ref, pn_ref, wg_ref, outp_ref, outs_ref, *, alpha, d_ff, n_prompt_tiles):
    on_prompt = pl.program_id(0) < n_prompt_tiles
    y = _read_stream(on_prompt, yp_ref, ys_ref)
    m = _mm(_read_stream(on_prompt, op_ref, os_ref), wo_ref[...])
    y1 = _layer_norm(alpha * y + m, ln1g_ref[...], ln1b_ref[...])
    y1b = y1.astype(BF16)
    acc = jnp.zeros_like(y1)
    for c in range(d_ff // FF_CHUNK):
        h = _mm(y1b, wup_ref[:, c * FF_CHUNK:(c + 1) * FF_CHUNK])
        h = jnp.square(jnp.maximum(h, 0.0))
        acc = acc + _mm(h, wdn_ref[c * FF_CHUNK:(c + 1) * FF_CHUNK, :])
    y2 = _layer_norm(alpha * y1 + acc, ln2g_ref[...], ln2b_ref[...])
    e = _rms_norm(_mm(_read_stream(on_prompt, pp_ref, ps_ref), wpp_ref[...]), pn_ref[...])
    gate = _sigmoid(_mm(y2, wg_ref[...]))
    _write_stream(on_prompt, outp_ref, outs_ref, y2 + gate * e)


def _finish_layer(y, o, p, layer, wo, ln1g, ln1b, wup, wdn, ln2g, ln2b, wpp, pn, wg, *, alpha, tm):
    (tp, d), ts = y[0].shape, y[1].shape[0]
    d_ff = wup.shape[1]
    ple = p[0].shape[1]
    n_p, n_s = tp // tm, ts // tm
    fix = lambda i: (0, 0)
    p_prompt_spec, _ = _stream_specs(tm, ple, n_p, first_tile=layer * n_p)
    p_sample_spec = pl.BlockSpec((tm, ple), lambda i: (layer * n_s + jnp.maximum(i - n_p, 0), 0))
    weights = 2 * (d * d * 2 + d * d_ff + d_ff * d + ple * d)
    tiles = 2 * 2 * (2 * tm * d * 4 + tm * d * 2 + tm * ple * 4)
    temps = 6 * tm * d * 4 + 2 * tm * FF_CHUNK * 4
    return pl.pallas_call(
        functools.partial(_finish_kernel, alpha=alpha, d_ff=d_ff, n_prompt_tiles=n_p),
        grid=(n_p + n_s,),
        in_specs=[
            *_stream_specs(tm, d, n_p), *_stream_specs(tm, d, n_p), p_prompt_spec, p_sample_spec,
            _resident((d, d), fix), _resident((1, d), fix), _resident((1, d), fix),
            _resident((d, d_ff), fix), _resident((d_ff, d), fix), _resident((1, d), fix), _resident((1, d), fix),
            _resident((ple, d), fix), _resident((1, d), fix), _resident((d, d), fix),
        ],
        out_specs=list(_stream_specs(tm, d, n_p)),
        out_shape=[jax.ShapeDtypeStruct((tp, d), F32), jax.ShapeDtypeStruct((ts, d), F32)],
        compiler_params=pltpu.CompilerParams(dimension_semantics=("arbitrary",),
                                             vmem_limit_bytes=_vmem_limit(weights + tiles + temps)),
        name="finish_layer",
    )(*y, *o, *p, wo, ln1g, ln1b, wup, wdn, ln2g, ln2b, wpp, pn, wg)


def _dn_proj_kernel(yp_ref, ys_ref, wqkv_ref, wz_ref, wba_ref, qkv_ref, z_ref, ba_ref, *, n_prompt_tiles):
    yb = _read_stream(pl.program_id(0) < n_prompt_tiles, yp_ref, ys_ref).astype(BF16)
    qkv_ref[...] = _mm(yb, wqkv_ref[...])
    z_ref[...] = _mm(yb, wz_ref[...])
    ba_ref[...] = _mm(yb, wba_ref[...])


def _dn_proj(y, wqkv, wz, wba, *, tm):
    (tp, d), ts = y[0].shape, y[1].shape[0]
    t = tp + ts
    nq, nz, nb = wqkv.shape[1], wz.shape[1], wba.shape[1]
    row = lambda i: (i, 0)
    fix = lambda i: (0, 0)
    weights = 2 * d * (nq + nz + nb)
    tiles = 2 * 4 * tm * (2 * d + nq + nz + nb)
    temps = 2 * tm * d + 4 * tm * nq
    return pl.pallas_call(
        functools.partial(_dn_proj_kernel, n_prompt_tiles=tp // tm),
        grid=(t // tm,),
        in_specs=[*_stream_specs(tm, d, tp // tm), _resident((d, nq), fix), _resident((d, nz), fix),
                  _resident((d, nb), fix)],
        out_specs=[pl.BlockSpec((tm, nq), row), pl.BlockSpec((tm, nz), row), pl.BlockSpec((tm, nb), row)],
        out_shape=[jax.ShapeDtypeStruct((t, nq), F32), jax.ShapeDtypeStruct((t, nz), F32),
                   jax.ShapeDtypeStruct((t, nb), F32)],
        compiler_params=pltpu.CompilerParams(dimension_semantics=("arbitrary",),
                                             vmem_limit_bytes=_vmem_limit(weights + tiles + temps)),
        name="dn_proj",
    )(*y, wqkv, wz, wba)


def _dn_core_kernel(qkv_ref, z_ref, ba_ref, cprev_ref, s0_ref, cw_ref, alog_ref, dtb_ref, onorm_ref,
                    o_ref, ctail_ref, sout_ref,
                    xs_ref, q_ref, k_ref, v_ref, s_ref, *, tb, c, n_steps):
    heads = DN_HEADS
    dk = DN_DK
    qk_dim = heads * dk
    g_heads = LANES // c
    n_groups = heads // g_heads
    step = pl.program_id(1)

    @pl.when(step == 0)
    def _():
        xs_ref[0:SUBLANES, :] = cprev_ref[...]
        s_ref[...] = s0_ref[...]

    x = qkv_ref[...]
    xs_ref[SUBLANES:SUBLANES + tb, :] = x
    cw = cw_ref[...]
    conv = x * cw[CONV_W - 1:CONV_W, :]
    for k in range(1, CONV_W):
        conv = conv + xs_ref[SUBLANES - k:SUBLANES - k + tb, :] * cw[CONV_W - 1 - k:CONV_W - k, :]
    tail = xs_ref[tb:tb + SUBLANES, :]
    xs_ref[0:SUBLANES, :] = tail

    @pl.when(step == n_steps - 1)
    def _():
        ctail_ref[...] = tail

    act = conv * _sigmoid(conv)
    for h in range(heads):
        qh = act[:, h * dk:(h + 1) * dk]
        kh = act[:, qk_dim + h * dk:qk_dim + (h + 1) * dk]
        q_ref[:, h * dk:(h + 1) * dk] = qh * lax.rsqrt(jnp.sum(qh * qh, axis=-1, keepdims=True) + EPS) * (dk ** -0.5)
        k_ref[:, h * dk:(h + 1) * dk] = kh * lax.rsqrt(jnp.sum(kh * kh, axis=-1, keepdims=True) + EPS)
    v_ref[...] = act[:, 2 * qk_dim:]

    rows = lax.broadcasted_iota(jnp.int32, (LANES, LANES), 0)
    cols = lax.broadcasted_iota(jnp.int32, (LANES, LANES), 1)
    same_block = (rows // c) == (cols // c)
    causal = same_block & (rows >= cols)
    strict = same_block & (rows > cols)
    eye = (rows == cols).astype(F32)
    lr = lax.broadcasted_iota(jnp.int32, (LANES, c), 0)
    lc = lax.broadcasted_iota(jnp.int32, (LANES, c), 1)
    cum_op = ((lc <= lr) & (lr < c)).astype(BF16)
    neg_exp_alog = -jnp.exp(alog_ref[...])
    dtb = dtb_ref[...]
    onorm = onorm_ref[...]

    def stack(pieces):
        return pieces[0] if len(pieces) == 1 else jnp.concatenate(pieces, axis=0)

    groups = [[p * g_heads + s for s in range(g_heads)] for p in range(n_groups)]

    n_chunks = tb // c
    par = DN_CHUNKS_PER_ITER if n_chunks % DN_CHUNKS_PER_ITER == 0 else 1
    problems = [(j, hs) for j in range(par) for hs in groups]

    def col(x, r_lo, r_hi, lane):
        return jnp.broadcast_to(x[r_lo:r_hi, lane:lane + 1], (c, LANES))

    def chunk_body(ci, carry):
        r0 = [pl.multiple_of((ci * par + j) * c, c) for j in range(par)]
        ba = [ba_ref[pl.ds(r, c), :] for r in r0]

        def gather(ref, width):
            return [stack([ref[pl.ds(r0[j], c), h * width:(h + 1) * width] for h in hs]) for j, hs in problems]

        qp, kp, vp, zp = gather(q_ref, dk), gather(k_ref, dk), gather(v_ref, DN_DV), gather(z_ref, DN_DV)
        state = [s_ref[h] for h in range(heads)]

        beta_all = [_sigmoid(x) for x in ba]
        g_all = [neg_exp_alog * _softplus(x + dtb) for x in ba]
        gc = [_mm_exact_lhs3(cum_op, g) for g in g_all]
        gct = [x.T for x in gc]
        exp_gc = [jnp.exp(x) for x in gc]

        col_gc = [stack([col(gc[j], 0, c, heads + h) for h in hs]) for j, hs in problems]
        col_eg = [stack([col(exp_gc[j], 0, c, heads + h) for h in hs]) for j, hs in problems]
        col_gl = [stack([col(gc[j], c - 1, c, heads + h) for h in hs]) for j, hs in problems]
        col_beta = [stack([col(beta_all[j], 0, c, h) for h in hs]) for j, hs in problems]
        decay = []
        for i, (j, hs) in enumerate(problems):
            row_gc = gct[j][heads + hs[0]:heads + hs[0] + 1, :]
            for s in range(1, g_heads):
                row_gc = row_gc + pltpu.roll(gct[j][heads + hs[s]:heads + hs[s] + 1, :], s * c, 1)
            decay.append(jnp.exp(jnp.where(causal, col_gc[i] - row_gc, -jnp.inf)))
        n_prob = len(problems)
        kb = [kp[i] * col_beta[i] for i in range(n_prob)]
        kq = [_mm_nt(jnp.concatenate([kb[i], qp[i]], axis=0), kp[i]) for i in range(n_prob)]
        a_low = [jnp.where(strict, kq[i][:LANES] * decay[i], 0.0) for i in range(n_prob)]
        attn = [kq[i][LANES:] * decay[i] for i in range(n_prob)]
        t_inv = [eye - a for a in a_low]
        pw = [_mm3(a, a) for a in a_low]
        for _ in range(int(math.log2(c)) - 2):
            both = [_mm3(jnp.concatenate([t, x], axis=0), x) for t, x in zip(t_inv, pw)]
            t_inv = [t + b[:LANES] for t, b in zip(t_inv, both)]
            pw = [b[LANES:] for b in both]
        t_inv = [t + _mm3(t, x) for t, x in zip(t_inv, pw)]
        uw = [_mm3(t_inv[i], jnp.concatenate([vp[i] * col_beta[i], kb[i] * col_eg[i]], axis=1))
              for i in range(n_prob)]
        qg = [qp[i] * col_eg[i] for i in range(n_prob)]
        kd = [kp[i] * jnp.exp(col_gl[i] - col_gc[i]) for i in range(n_prob)]
        outs = []
        for i, (j, hs) in enumerate(problems):
            u, w = uw[i][:, :DN_DV], uw[i][:, DN_DV:]
            ws, qs = [], []
            for s, h in enumerate(hs):
                wq = _mm(jnp.concatenate([w[s * c:(s + 1) * c], qg[i][s * c:(s + 1) * c]], axis=0), state[h])
                ws.append(wq[:c])
                qs.append(wq[c:])
            v_new = u - stack(ws)
            o = stack(qs) + _mm(attn[i], v_new)
            kd_heads = jnp.concatenate([jnp.where((rows // c) == s, kd[i], 0.0) for s in range(g_heads)], axis=1)
            kv = _mm_tn(kd_heads, v_new)
            for s, h in enumerate(hs):
                decay_last = jnp.exp(gc[j][c - 1:c, heads + h:heads + h + 1])
                state[h] = state[h] * decay_last + kv[s * DN_DK:(s + 1) * DN_DK]
            o = o * lax.rsqrt(jnp.mean(o * o, axis=-1, keepdims=True) + EPS) * onorm
            outs.append(o * (zp[i] * _sigmoid(zp[i])))

        for h in range(heads):
            s_ref[h] = state[h]
        for i, (j, hs) in enumerate(problems):
            for s, h in enumerate(hs):
                o_ref[pl.ds(r0[j], c), h * DN_DV:(h + 1) * DN_DV] = outs[i][s * c:(s + 1) * c].astype(o_ref.dtype)
        return carry

    lax.fori_loop(0, n_chunks // par, chunk_body, 0)

    @pl.when(step == n_steps - 1)
    def _():
        sout_ref[...] = s_ref[...]


def _dn_core(qkv, z, ba, conv_prev, s0, conv_w, alog, dtb, onorm, *, n_seq, seq_len, row0, tb, c):
    heads = DN_HEADS
    n_steps = seq_len // tb
    blk0 = row0 // tb
    cdim = qkv.shape[1]
    vdim = z.shape[1]
    tok = lambda b, l: (blk0 + b * n_steps + l, 0)
    seq3 = lambda b, l: (b, 0, 0)
    seq4 = lambda b, l: (b, 0, 0, 0)
    fix = lambda b, l: (0, 0)
    tiles = 2 * tb * (4 * cdim + 4 * vdim + 4 * LANES + 2 * vdim)
    state = 5 * heads * DN_DK * DN_DV * 4 + 4 * SUBLANES * cdim * 4
    scratch = (tb + SUBLANES) * cdim * 4 + 3 * tb * vdim * 4
    temps = 3 * tb * cdim * 4
    return pl.pallas_call(
        functools.partial(_dn_core_kernel, tb=tb, c=c, n_steps=n_steps),
        grid=(n_seq, n_steps),
        in_specs=[
            pl.BlockSpec((tb, cdim), tok), pl.BlockSpec((tb, vdim), tok), pl.BlockSpec((tb, LANES), tok),
            pl.BlockSpec((None, SUBLANES, cdim), seq3), pl.BlockSpec((None, heads, DN_DK, DN_DV), seq4),
            _resident((SUBLANES, cdim), fix), _resident((1, LANES), fix), _resident((1, LANES), fix),
            _resident((1, LANES), fix),
        ],
        out_specs=[
            pl.BlockSpec((tb, vdim), lambda b, l: (b * n_steps + l, 0)),
            pl.BlockSpec((None, SUBLANES, cdim), seq3),
            pl.BlockSpec((None, heads, DN_DK, DN_DV), seq4),
        ],
        out_shape=[
            jax.ShapeDtypeStruct((n_seq * seq_len, vdim), BF16),
            jax.ShapeDtypeStruct((n_seq, SUBLANES, cdim), F32),
            jax.ShapeDtypeStruct((n_seq, heads, DN_DK, DN_DV), F32),
        ],
        scratch_shapes=[
            pltpu.VMEM((tb + SUBLANES, cdim), F32),
            pltpu.VMEM((tb, vdim), F32), pltpu.VMEM((tb, vdim), F32), pltpu.VMEM((tb, vdim), F32),
            pltpu.VMEM((heads, DN_DK, DN_DV), F32),
        ],
        compiler_params=pltpu.CompilerParams(dimension_semantics=("arbitrary", "arbitrary"),
                                             vmem_limit_bytes=_vmem_limit(tiles + state + scratch + temps)),
        name="dn_core",
    )(qkv, z, ba, conv_prev, s0, conv_w, alog, dtb, onorm)


def _rope128(x, cos, sin_lo, sin_hi):
    half = ROPE_DIM // 2
    return x * cos + pltpu.roll(x, LANES - half, 1) * sin_lo + pltpu.roll(x, half, 1) * sin_hi


def _mla_proj_kernel(yp_ref, ys_ref, win_ref, qn_ref, kvn_ref, wqn_ref, wqr_ref, wuk_ref, wuv_ref,
                     cos_ref, sl_ref, sh_ref, q_ref, k_ref, v_ref, ckvp_ref, ckvs_ref, krp_ref, krs_ref,
                     *, n_prompt_tiles):
    heads = MLA_HEADS
    on_prompt = pl.program_id(0) < n_prompt_tiles
    proj = _mm(_read_stream(on_prompt, yp_ref, ys_ref), win_ref[...])
    cq = _rms_norm(proj[:, :Q_LORA], qn_ref[...]).astype(BF16)
    ckv = _rms_norm(proj[:, Q_LORA:Q_LORA + KV_LORA], kvn_ref[...])
    cos, sl, sh = cos_ref[...], sl_ref[...], sh_ref[...]
    kr = _rope128(proj[:, Q_LORA + KV_LORA:], cos, sl, sh)
    _write_stream(on_prompt, ckvp_ref, ckvs_ref, ckv)
    _write_stream(on_prompt, krp_ref, krs_ref, kr[:, :ROPE_DIM])
    ckvb = ckv.astype(BF16)
    qn = _mm(cq, wqn_ref[...])
    qr = _mm(cq, wqr_ref[...])
    kn = _mm(ckvb, wuk_ref[...])
    v_ref[...] = _mm(ckvb, wuv_ref[...]).astype(BF16)
    krb = kr.astype(BF16)
    for h in range(heads):
        lo = h * QK_PAD
        q_ref[:, lo:lo + LANES] = (qn[:, h * LANES:(h + 1) * LANES] * Q_PRESCALE).astype(BF16)
        q_ref[:, lo + LANES:lo + QK_PAD] = (
            _rope128(qr[:, h * LANES:(h + 1) * LANES], cos, sl, sh) * Q_PRESCALE).astype(BF16)
        k_ref[:, lo:lo + LANES] = kn[:, h * LANES:(h + 1) * LANES].astype(BF16)
        k_ref[:, lo + LANES:lo + QK_PAD] = krb


def _mla_proj(y, win, qn, kvn, wqn, wqr, wuk, wuv, cos, sl, sh, *, tm):
    (tp, d), ts = y[0].shape, y[1].shape[0]
    t = tp + ts
    heads = MLA_HEADS
    nin = win.shape[1]
    n_p = tp // tm
    row = lambda i: (i, 0)
    fix = lambda i: (0, 0)
    weights = 2 * (d * nin + Q_LORA * 2 * heads * LANES + KV_LORA * 2 * heads * LANES)
    tiles = 2 * tm * (2 * 4 * d + 3 * 4 * LANES + 2 * 2 * heads * QK_PAD + 2 * heads * V_DIM
                      + 2 * 4 * KV_LORA + 2 * 4 * LANES)
    temps = tm * (4 * nin + 4 * 4 * heads * LANES + 2 * d)
    return pl.pallas_call(
        functools.partial(_mla_proj_kernel, n_prompt_tiles=n_p),
        grid=(t // tm,),
        in_specs=[
            *_stream_specs(tm, d, n_p), _resident((d, nin), fix), _resident((1, Q_LORA), fix),
            _resident((1, KV_LORA), fix), _resident((Q_LORA, heads * LANES), fix),
            _resident((Q_LORA, heads * LANES), fix), _resident((KV_LORA, heads * NOPE_DIM), fix),
            _resident((KV_LORA, heads * V_DIM), fix),
            pl.BlockSpec((tm, LANES), row), pl.BlockSpec((tm, LANES), row), pl.BlockSpec((tm, LANES), row),
        ],
        out_specs=[
            pl.BlockSpec((tm, heads * QK_PAD), row), pl.BlockSpec((tm, heads * QK_PAD), row),
            pl.BlockSpec((tm, heads * V_DIM), row),
            *_stream_specs(tm, KV_LORA, n_p), *_stream_specs(tm, ROPE_DIM, n_p),
        ],
        out_shape=[
            jax.ShapeDtypeStruct((t, heads * QK_PAD), BF16), jax.ShapeDtypeStruct((t, heads * QK_PAD), BF16),
            jax.ShapeDtypeStruct((t, heads * V_DIM), BF16),
            jax.ShapeDtypeStruct((tp, KV_LORA), F32), jax.ShapeDtypeStruct((ts, KV_LORA), F32),
            jax.ShapeDtypeStruct((tp, ROPE_DIM), F32), jax.ShapeDtypeStruct((ts, ROPE_DIM), F32),
        ],
        compiler_params=pltpu.CompilerParams(dimension_semantics=("arbitrary",),
                                             vmem_limit_bytes=_vmem_limit(weights + tiles + temps)),
        name="mla_proj",
    )(*y, win, qn, kvn, wqn, wqr, wuk, wuv, cos, sl, sh)


def _flash_kernel(q_ref, k_ref, v_ref, o_ref, m_ref, l_ref, acc_ref, *, tq, tk):
    i = pl.program_id(2)
    n_sub = tq // tk
    reps = tk // LANES

    def online_softmax(stats, q, kj, vj, mask):
        m_prev, l_prev, acc = stats
        s = _mm_nt(q, kj)
        if mask is not None:
            s = jnp.where(mask, s, -jnp.inf)
        m_new = jnp.maximum(m_prev, jnp.max(s, axis=-1, keepdims=True))
        alpha = jnp.exp2(m_prev - m_new)
        p = jnp.exp2(s - jnp.concatenate([m_new] * reps, axis=1))
        l_new = alpha * l_prev + jnp.sum(p, axis=-1, keepdims=True)
        return m_new, l_new, alpha * acc + _mm(p, vj)

    m_ref[...] = jnp.full(m_ref.shape, -jnp.inf, F32)
    l_ref[...] = jnp.zeros(l_ref.shape, F32)
    acc_ref[...] = jnp.zeros(acc_ref.shape, F32)

    def body(j, carry):
        q = q_ref[...]
        tiles = []
        for u in range(n_sub):
            k0 = pl.multiple_of((j * n_sub + u) * tk, tk)
            tiles.append((k_ref[pl.ds(k0, tk), :], v_ref[pl.ds(k0, tk), :]))
        stats = (m_ref[...], l_ref[...], acc_ref[...])
        for kj, vj in tiles:
            stats = online_softmax(stats, q, kj, vj, None)
        m_ref[...], l_ref[...], acc_ref[...] = stats
        return carry

    lax.fori_loop(0, i, body, 0)

    for u in range(n_sub):
        k0 = pl.multiple_of(i * tq + u * tk, tk)
        rows = slice(u * tk, tq)
        nr = tq - u * tk
        qc = (lax.broadcasted_iota(jnp.int32, (nr, tk), 0) + u * tk) // CHUNK
        kc = (lax.broadcasted_iota(jnp.int32, (nr, tk), 1) + u * tk) // CHUNK
        stats = (m_ref[rows, :], l_ref[rows, :], acc_ref[rows, :])
        stats = online_softmax(stats, q_ref[rows, :], k_ref[pl.ds(k0, tk), :], v_ref[pl.ds(k0, tk), :], kc <= qc)
        m_ref[rows, :], l_ref[rows, :], acc_ref[rows, :] = stats
    o_ref[...] = (acc_ref[...] / l_ref[...]).astype(o_ref.dtype)


def _flash_attention(q, k, v, *, n_seq, seq_len, tq, tk):
    heads = MLA_HEADS
    nq = seq_len // tq
    blocks = 2 * (tq * QK_PAD * 2 + seq_len * QK_PAD * 2 + seq_len * V_DIM * 2 + tq * V_DIM * 2)
    scratch = 3 * tq * LANES * 4
    temps = (tq // tk + 2) * tq * tk * 4
    return pl.pallas_call(
        functools.partial(_flash_kernel, tq=tq, tk=tk),
        grid=(n_seq, heads, nq),
        in_specs=[
            pl.BlockSpec((tq, QK_PAD), lambda b, h, i: (b * nq + i, h)),
            pl.BlockSpec((seq_len, QK_PAD), lambda b, h, i: (b, h)),
            pl.BlockSpec((seq_len, V_DIM), lambda b, h, i: (b, h)),
        ],
        out_specs=pl.BlockSpec((tq, V_DIM), lambda b, h, i: (b * nq + i, h)),
        out_shape=jax.ShapeDtypeStruct((n_seq * seq_len, heads * V_DIM), BF16),
        scratch_shapes=[pltpu.VMEM((tq, LANES), F32), pltpu.VMEM((tq, LANES), F32), pltpu.VMEM((tq, V_DIM), F32)],
        compiler_params=pltpu.CompilerParams(dimension_semantics=("arbitrary", "arbitrary", "arbitrary"),
                                             vmem_limit_bytes=_vmem_limit(blocks + scratch + temps)),
        name="mla_flash",
    )(q, k, v)


def _decode_attn_kernel(q_ref, ckv_c_ref, kr_c_ref, ckv_n_ref, kr_n_ref, wuk_ref, wuv_ref, o_ref, *, past, seq):
    heads = MLA_HEADS
    qall = q_ref[...]
    qlat, qrope = [], []
    for h in range(heads):
        qn = qall[:, h * QK_PAD:h * QK_PAD + LANES]
        qlat.append(_mm_nt(qn, wuk_ref[h]).astype(BF16))
        qrope.append(qall[:, h * QK_PAD + LANES:(h + 1) * QK_PAD])
    qlat = jnp.concatenate(qlat, axis=0)
    qrope = jnp.concatenate(qrope, axis=0)
    n_rows = heads * seq
    n_keys = past + seq
    pad = (-n_keys) % LANES
    ckv_parts = [ckv_c_ref[...].astype(BF16), ckv_n_ref[...].astype(BF16)]
    kr_parts = [kr_c_ref[...].astype(BF16), kr_n_ref[...].astype(BF16)]
    if pad:
        ckv_parts.append(jnp.zeros((pad, KV_LORA), BF16))
        kr_parts.append(jnp.zeros((pad, ROPE_DIM), BF16))
    ckv_all = jnp.concatenate(ckv_parts, axis=0)
    kr_all = jnp.concatenate(kr_parts, axis=0)
    kr_all = jnp.concatenate([kr_all, jnp.zeros((n_keys + pad, LANES - ROPE_DIM), BF16)], axis=1)
    s = _mm_nt(qlat, ckv_all) + _mm_nt(qrope, kr_all)
    q_pos = past + lax.broadcasted_iota(jnp.int32, (n_rows, n_keys + pad), 0) % seq
    k_pos = lax.broadcasted_iota(jnp.int32, (n_rows, n_keys + pad), 1)
    visible = ((k_pos // CHUNK) <= (q_pos // CHUNK)) & (k_pos < n_keys)
    s = jnp.where(visible, s, -jnp.inf)
    p = jnp.exp2(s - jnp.max(s, axis=-1, keepdims=True))
    p = p / jnp.sum(p, axis=-1, keepdims=True)
    o_lat = _mm(p, ckv_all)
    for h in range(heads):
        o_ref[:, h * V_DIM:(h + 1) * V_DIM] = _mm(o_lat[h * seq:(h + 1) * seq], wuv_ref[h]).astype(o_ref.dtype)


def _decode_attention(q, ckv_cache, kr_cache, ckv, kr, wuk_h, wuv_h, *, n_seq, seq, row0):
    heads = MLA_HEADS
    past = ckv_cache.shape[1]
    blk0 = row0 // seq
    tok = lambda b: (blk0 + b, 0)
    new = lambda b: (b, 0)
    fix3 = lambda b: (0, 0, 0)
    blocks = 2 * (seq * heads * QK_PAD * 2 + past * (KV_LORA + LANES) * 4 + seq * (KV_LORA + LANES) * 4
                  + seq * heads * V_DIM * 2) + 2 * 2 * heads * KV_LORA * LANES * 2
    temps = (past + seq + LANES) * (KV_LORA + LANES) * 2 * 2 + 6 * heads * seq * (past + seq + LANES) * 4
    return pl.pallas_call(
        functools.partial(_decode_attn_kernel, past=past, seq=seq),
        grid=(n_seq,),
        in_specs=[
            pl.BlockSpec((seq, heads * QK_PAD), tok),
            pl.BlockSpec((None, past, KV_LORA), lambda b: (b, 0, 0)),
            pl.BlockSpec((None, past, ROPE_DIM), lambda b: (b, 0, 0)),
            pl.BlockSpec((seq, KV_LORA), new), pl.BlockSpec((seq, ROPE_DIM), new),
            _resident((heads, KV_LORA, NOPE_DIM), fix3), _resident((heads, KV_LORA, V_DIM), fix3),
        ],
        out_specs=pl.BlockSpec((seq, heads * V_DIM), lambda b: (b, 0)),
        out_shape=jax.ShapeDtypeStruct((n_seq * seq, heads * V_DIM), BF16),
        compiler_params=pltpu.CompilerParams(dimension_semantics=("arbitrary",),
                                             vmem_limit_bytes=_vmem_limit(blocks + temps)),
        name="mla_decode",
    )(q, ckv_cache, kr_cache, ckv, kr, wuk_h, wuv_h)


def _largest_tile(limit, *sizes):
    t = limit
    while any(s % t for s in sizes):
        t //= 2
    return t


def _rope_tables(pos):
    half = ROPE_DIM // 2
    inv_freq = ROPE_THETA ** (-jnp.arange(half, dtype=F32) / half)
    ang = pos.astype(F32)[:, None] * inv_freq[None, :]
    cos, sin = jnp.cos(ang), jnp.sin(ang)
    zero = jnp.zeros_like(cos)
    return (jnp.concatenate([cos, cos, zero, zero], axis=1), jnp.concatenate([-sin, zero, zero, zero], axis=1),
            jnp.concatenate([zero, sin, zero, zero], axis=1))


def _pad_cols(w, width):
    return jnp.pad(w, ((0, 0), (0, width - w.shape[1])))


def kernel(x_prompt, x_sample, state_dn_conv, state_dn_recurrent, cache_mla_ckv, cache_mla_krope, p_prompt, p_sample, ln1_g, ln1_b, ln2_g, ln2_b, mlp_w_up, mlp_w_down, ple_w_proj, ple_norm, ple_w_gate, dn_w_in, dn_conv_w, dn_a_log, dn_dt_bias, dn_o_norm, dn_w_o, mla_w_in, mla_q_norm, mla_w_uq, mla_kv_norm, mla_w_uk, mla_w_uv, mla_w_o):
    bp, lp, d = x_prompt.shape
    bs, ls, _ = x_sample.shape
    depth = ln1_g.shape[0]
    past = cache_mla_ckv.shape[2]
    tp, ts = bp * lp, bs * ls
    alpha = (2 * depth) ** 0.25
    tm = _largest_tile(TOKEN_TILE, tp, ts)
    heads = DN_HEADS
    qk_dim = heads * DN_DK
    conv_dim = 2 * qk_dim + heads * DN_DV
    v_dim = heads * DN_DV

    y = (x_prompt.reshape(tp, d), x_sample.reshape(ts, d))
    p = (p_prompt.reshape(depth * tp, -1), p_sample.reshape(depth * ts, -1))
    row = lambda a: a.reshape(1, -1)

    pos = jnp.concatenate([jnp.tile(jnp.arange(lp), bp), jnp.tile(past + jnp.arange(ls), bs)])
    cos_t, sin_lo_t, sin_hi_t = _rope_tables(pos)

    p_conv, p_rec, p_ckv, p_kr = [], [], [], []
    s_conv, s_rec, s_ckv, s_kr = [], [], [], []
    for i in range(depth):
        j = i // N_MIXERS
        if i % N_MIXERS == 0:
            w_in = dn_w_in[j]
            wqkv = w_in[:, :conv_dim].astype(BF16)
            wz = w_in[:, conv_dim:conv_dim + v_dim].astype(BF16)
            wba = _pad_cols(w_in[:, conv_dim + v_dim:], LANES).astype(BF16)
            qkv, z, ba = _dn_proj(y, wqkv, wz, wba, tm=tm)
            cw = jnp.pad(dn_conv_w[j], ((0, SUBLANES - CONV_W), (0, 0)))
            alog = jnp.pad(dn_a_log[j], (heads, LANES - 2 * heads)).reshape(1, LANES)
            dtb = jnp.pad(dn_dt_bias[j], (heads, LANES - 2 * heads)).reshape(1, LANES)
            onorm = row(dn_o_norm[j])
            hist = lambda st: jnp.pad(st, ((0, 0), (SUBLANES - (CONV_W - 1), 0), (0, 0)))
            cp = min(CHUNK, lp)
            o_p, ct_p, st_p = _dn_core(
                qkv, z, ba, jnp.zeros((bp, SUBLANES, conv_dim), F32), jnp.zeros((bp, heads, DN_DK, DN_DV), F32),
                cw, alog, dtb, onorm, n_seq=bp, seq_len=lp, row0=0, tb=_largest_tile(DN_STEP_TOKENS, lp), c=cp)
            cs = min(CHUNK, ls)
            o_s, ct_s, st_s = _dn_core(
                qkv, z, ba, hist(state_dn_conv[j]), state_dn_recurrent[j],
                cw, alog, dtb, onorm, n_seq=bs, seq_len=ls, row0=tp, tb=ls, c=cs)
            p_conv.append(ct_p[:, SUBLANES - (CONV_W - 1):])
            p_rec.append(st_p)
            s_conv.append(ct_s[:, SUBLANES - (CONV_W - 1):])
            s_rec.append(st_s)
            w_o = dn_w_o[j]
        else:
            w_in = _pad_cols(mla_w_in[j], Q_LORA + KV_LORA + LANES).astype(BF16)
            w_uq = mla_w_uq[j].reshape(Q_LORA, MLA_HEADS, NOPE_DIM + ROPE_DIM)
            wqn = w_uq[:, :, :NOPE_DIM].reshape(Q_LORA, MLA_HEADS * NOPE_DIM).astype(BF16)
            wqr = jnp.pad(w_uq[:, :, NOPE_DIM:], ((0, 0), (0, 0), (0, LANES - ROPE_DIM))
                          ).reshape(Q_LORA, MLA_HEADS * LANES).astype(BF16)
            wuk = mla_w_uk[j].astype(BF16)
            wuv = mla_w_uv[j].astype(BF16)
            q, k, v, ckv_p, ckv_s, kr_p, kr_s = _mla_proj(
                y, w_in, row(mla_q_norm[j]), row(mla_kv_norm[j]), wqn, wqr,
                wuk.reshape(KV_LORA, MLA_HEADS * NOPE_DIM), wuv.reshape(KV_LORA, MLA_HEADS * V_DIM),
                cos_t, sin_lo_t, sin_hi_t, tm=tm)
            tq = _largest_tile(ATTN_Q_TILE, lp)
            o_p = _flash_attention(q, k, v, n_seq=bp, seq_len=lp, tq=tq, tk=min(ATTN_K_TILE, tq))
            o_s = _decode_attention(q, cache_mla_ckv[j], cache_mla_krope[j], ckv_s, kr_s,
                                    jnp.transpose(wuk, (1, 0, 2)), jnp.transpose(wuv, (1, 0, 2)),
                                    n_seq=bs, seq=ls, row0=tp)
            p_ckv.append(ckv_p.reshape(bp, lp, KV_LORA))
            p_kr.append(kr_p.reshape(bp, lp, ROPE_DIM))
            s_ckv.append(ckv_s.reshape(bs, ls, KV_LORA))
            s_kr.append(kr_s.reshape(bs, ls, ROPE_DIM))
            w_o = mla_w_o[j]
        y = _finish_layer(y, (o_p, o_s), p, i, w_o.astype(BF16), row(ln1_g[i]), row(ln1_b[i]),
                          mlp_w_up[i].astype(BF16), mlp_w_down[i].astype(BF16), row(ln2_g[i]), row(ln2_b[i]),
                          ple_w_proj[i].astype(BF16), row(ple_norm[i]), ple_w_gate[i].astype(BF16),
                          alpha=alpha, tm=tm)
    return (y[0].reshape(bp, lp, d), y[1].reshape(bs, ls, d),
            jnp.stack(p_conv), jnp.stack(p_rec), jnp.stack(p_ckv), jnp.stack(p_kr),
            jnp.stack(s_conv), jnp.stack(s_rec), jnp.stack(s_ckv), jnp.stack(s_kr))
```

```python
import functools
import math

import jax
import jax.numpy as jnp
from jax import lax
from jax.experimental import pallas as pl
from jax.experimental.pallas import tpu as pltpu

F32 = jnp.float32
BF16 = jnp.bfloat16

CHUNK = 64
N_MIXERS = 2
EPS = 1e-6
DN_HEADS = 8
DN_DK = 128
DN_DV = 128
CONV_W = 4
MLA_HEADS = 8
Q_LORA = 512
KV_LORA = 256
NOPE_DIM = 128
ROPE_DIM = 64
V_DIM = 128
MLA_SCALE = (NOPE_DIM + ROPE_DIM) ** -0.5
ROPE_THETA = 10000.0
Q_PRESCALE = MLA_SCALE * math.log2(math.e)

LANES = 128
SUBLANES = 8
VMEM_BUDGET_BYTES = 60000 * 1024

TOKEN_TILE = 512
FF_CHUNK = 1024
DN_STEP_TOKENS = 256
DN_CHUNKS_PER_ITER = 4
ATTN_Q_TILE = 2048
ATTN_K_TILE = 512
QK_PAD = 2 * LANES


def _vmem_limit(nbytes):
    return int(min(VMEM_BUDGET_BYTES, nbytes))


def _nbytes(shape, dtype):
    return math.prod(shape) * jnp.dtype(dtype).itemsize


def _resident(shape, layer):
    index = (layer,) + (0,) * len(shape)
    return pl.BlockSpec((None, *shape), lambda *_: index, pipeline_mode=pl.Buffered(1))


def _mm(a, b):
    return jnp.dot(a.astype(BF16), b.astype(BF16), preferred_element_type=F32)


def _mm_nt(a, b):
    return lax.dot_general(a.astype(BF16), b.astype(BF16), (((1,), (1,)), ((), ())),
                           preferred_element_type=F32)


def _mm_tn(a, b):
    return pl.dot(a.astype(BF16), b.astype(BF16), trans_a=True)


def _split2(x):
    hi = x.astype(BF16)
    lo = (x - hi.astype(F32)).astype(BF16)
    return hi, lo


def _mm3(a, b):
    ah, al = _split2(a)
    bh, bl = _split2(b)
    n = b.shape[1]
    rhs = jnp.concatenate([jnp.concatenate([bh, bl], axis=1),
                           jnp.concatenate([bh, jnp.zeros_like(bl)], axis=1)], axis=0)
    r = jnp.dot(jnp.concatenate([ah, al], axis=1), rhs, preferred_element_type=F32)
    return r[:, :n] + r[:, n:]


def _mm_exact_lhs3(a_bf16, b):
    b0 = b.astype(BF16)
    r1 = b - b0.astype(F32)
    b1 = r1.astype(BF16)
    b2 = (r1 - b1.astype(F32)).astype(BF16)
    return _mm(a_bf16, b0) + (_mm(a_bf16, b1) + _mm(a_bf16, b2))


def _layer_norm(x, g, b):
    xc = x - jnp.mean(x, axis=-1, keepdims=True)
    var = jnp.mean(xc * xc, axis=-1, keepdims=True)
    return xc * lax.rsqrt(var + EPS) * g + b


def _rms_norm(x, g):
    return x * lax.rsqrt(jnp.mean(x * x, axis=-1, keepdims=True) + EPS) * g


def _sigmoid(x):
    return 1.0 / (1.0 + jnp.exp(-x))


def _softplus(x):
    return jnp.maximum(x, 0.0) + jnp.log1p(jnp.exp(-jnp.abs(x)))


def _stream_specs(tm, width, n_prompt_tiles, first_tile=0):
    prompt = pl.BlockSpec((tm, width), lambda i: (first_tile + jnp.minimum(i, n_prompt_tiles - 1), 0))
    sample = pl.BlockSpec((tm, width), lambda i: (jnp.maximum(i - n_prompt_tiles, 0), 0))
    return prompt, sample


def _on_active_stream(n_prompt_tiles, body, prompt_refs, sample_refs):
    on_prompt = pl.program_id(0) < n_prompt_tiles

    @pl.when(on_prompt)
    def _():
        body(*prompt_refs)

    @pl.when(jnp.logical_not(on_prompt))
    def _():
        body(*sample_refs)


def _finish_kernel(yp_ref, ys_ref, op_ref, os_ref, pp_ref, ps_ref, wo_ref, ln1g_ref, ln1b_ref, wup_ref, wdn_ref,
                   ln2g_ref, ln2b_ref, wpp_ref, pn_ref, wg_ref, outp_ref, outs_ref, *, alpha, d_ff, n_prompt_tiles):
    def tail(y_ref, o_ref, p_ref, out_ref):
        m = _mm(o_ref[...], wo_ref[...])
        y1 = _layer_norm(alpha * y_ref[...] + m, ln1g_ref[...], ln1b_ref[...])
        y1b = y1.astype(BF16)
        acc = jnp.zeros_like(y1)
        for c in range(d_ff // FF_CHUNK):
            h = _mm(y1b, wup_ref[:, c * FF_CHUNK:(c + 1) * FF_CHUNK])
            h = jnp.square(jnp.maximum(h, 0.0))
            acc = acc + _mm(h, wdn_ref[c * FF_CHUNK:(c + 1) * FF_CHUNK, :])
        y2 = _layer_norm(alpha * y1 + acc, ln2g_ref[...], ln2b_ref[...])
        e = _rms_norm(_mm(p_ref[...], wpp_ref[...]), pn_ref[...])
        gate = _sigmoid(_mm(y2, wg_ref[...]))
        out_ref[...] = y2 + gate * e

    _on_active_stream(n_prompt_tiles, tail, (yp_ref, op_ref, pp_ref, outp_ref), (ys_ref, os_ref, ps_ref, outs_ref))


def _finish_layer(y, o, p, layer, wo, wo_layer, ln1g, ln1b, wup, wdn, ln2g, ln2b, wpp, pn, wg, *, alpha, tm):
    (tp, d), ts = y[0].shape, y[1].shape[0]
    d_ff = wup.shape[2]
    ple = p[0].shape[1]
    n_p, n_s = tp // tm, ts // tm
    fix = layer
    p_prompt_spec, _ = _stream_specs(tm, ple, n_p, first_tile=layer * n_p)
    p_sample_spec = pl.BlockSpec((tm, ple), lambda i: (layer * n_s + jnp.maximum(i - n_p, 0), 0))
    weights = 2 * (d * d * 2 + d * d_ff + d_ff * d + ple * d)
    tiles = 2 * 2 * (2 * tm * d * 4 + tm * d * 2 + tm * ple * 4)
    temps = 6 * tm * d * 4 + 2 * tm * FF_CHUNK * 4
    return pl.pallas_call(
        functools.partial(_finish_kernel, alpha=alpha, d_ff=d_ff, n_prompt_tiles=n_p),
        grid=(n_p + n_s,),
        in_specs=[
            *_stream_specs(tm, d, n_p), *_stream_specs(tm, d, n_p), p_prompt_spec, p_sample_spec,
            _resident((d, d), wo_layer), _resident((1, d), fix), _resident((1, d), fix),
            _resident((d, d_ff), fix), _resident((d_ff, d), fix), _resident((1, d), fix), _resident((1, d), fix),
            _resident((ple, d), fix), _resident((1, d), fix), _resident((d, d), fix),
        ],
        out_specs=list(_stream_specs(tm, d, n_p)),
        out_shape=[jax.ShapeDtypeStruct((tp, d), F32), jax.ShapeDtypeStruct((ts, d), F32)],
        compiler_params=pltpu.CompilerParams(dimension_semantics=("arbitrary",),
                                             vmem_limit_bytes=_vmem_limit(weights + tiles + temps)),
        name="finish_layer",
    )(*y, *o, *p, wo, ln1g, ln1b, wup, wdn, ln2g, ln2b, wpp, pn, wg)


def _dn_proj_kernel(yp_ref, ys_ref, wqkv_ref, wz_ref, wba_ref, qkv_ref, z_ref, ba_ref, *, n_prompt_tiles):
    def project(y_ref):
        yb = y_ref[...].astype(BF16)
        qkv_ref[...] = _mm(yb, wqkv_ref[...])
        z_ref[...] = _mm(yb, wz_ref[...])
        ba_ref[...] = _mm(yb, wba_ref[...])

    _on_active_stream(n_prompt_tiles, project, (yp_ref,), (ys_ref,))


def _dn_proj(y, layer, wqkv, wz, wba, *, tm):
    (tp, d), ts = y[0].shape, y[1].shape[0]
    t = tp + ts
    nq, nz, nb = wqkv.shape[2], wz.shape[2], wba.shape[2]
    row = lambda i: (i, 0)
    fix = layer
    weights = 2 * d * (nq + nz + nb)
    tiles = 2 * 4 * tm * (2 * d + nq + nz + nb)
    temps = 2 * tm * d + 4 * tm * nq
    return pl.pallas_call(
        functools.partial(_dn_proj_kernel, n_prompt_tiles=tp // tm),
        grid=(t // tm,),
        in_specs=[*_stream_specs(tm, d, tp // tm), _resident((d, nq), fix), _resident((d, nz), fix),
                  _resident((d, nb), fix)],
        out_specs=[pl.BlockSpec((tm, nq), row), pl.BlockSpec((tm, nz), row), pl.BlockSpec((tm, nb), row)],
        out_shape=[jax.ShapeDtypeStruct((t, nq), F32), jax.ShapeDtypeStruct((t, nz), F32),
                   jax.ShapeDtypeStruct((t, nb), F32)],
        compiler_params=pltpu.CompilerParams(dimension_semantics=("arbitrary",),
                                             vmem_limit_bytes=_vmem_limit(weights + tiles + temps)),
        name="dn_proj",
    )(*y, wqkv, wz, wba)


def _dn_core_kernel(qkv_ref, z_ref, ba_ref, cprev_ref, s0_ref, cw_ref, alog_ref, dtb_ref, onorm_ref,
                    o_ref, ctail_ref, sout_ref,
                    xs_ref, q_ref, k_ref, v_ref, s_ref, *, tb, c, n_steps):
    heads = DN_HEADS
    dk = DN_DK
    qk_dim = heads * dk
    g_heads = LANES // c
    n_groups = heads // g_heads
    step = pl.program_id(1)

    @pl.when(step == 0)
    def _():
        xs_ref[0:SUBLANES, :] = cprev_ref[...]
        s_ref[...] = s0_ref[...]

    x = qkv_ref[...]
    xs_ref[SUBLANES:SUBLANES + tb, :] = x
    cw = cw_ref[...]
    conv = x * cw[CONV_W - 1:CONV_W, :]
    for k in range(1, CONV_W):
        conv = conv + xs_ref[SUBLANES - k:SUBLANES - k + tb, :] * cw[CONV_W - 1 - k:CONV_W - k, :]
    tail = xs_ref[tb:tb + SUBLANES, :]
    xs_ref[0:SUBLANES, :] = tail

    @pl.when(step == n_steps - 1)
    def _():
        ctail_ref[...] = tail

    act = conv * _sigmoid(conv)
    for h in range(heads):
        qh = act[:, h * dk:(h + 1) * dk]
        kh = act[:, qk_dim + h * dk:qk_dim + (h + 1) * dk]
        q_ref[:, h * dk:(h + 1) * dk] = qh * lax.rsqrt(jnp.sum(qh * qh, axis=-1, keepdims=True) + EPS) * (dk ** -0.5)
        k_ref[:, h * dk:(h + 1) * dk] = kh * lax.rsqrt(jnp.sum(kh * kh, axis=-1, keepdims=True) + EPS)
    v_ref[...] = act[:, 2 * qk_dim:]

    rows = lax.broadcasted_iota(jnp.int32, (LANES, LANES), 0)
    cols = lax.broadcasted_iota(jnp.int32, (LANES, LANES), 1)
    same_block = (rows // c) == (cols // c)
    causal = same_block & (rows >= cols)
    strict = same_block & (rows > cols)
    eye = (rows == cols).astype(F32)
    lr = lax.broadcasted_iota(jnp.int32, (LANES, c), 0)
    lc = lax.broadcasted_iota(jnp.int32, (LANES, c), 1)
    cum_op = ((lc <= lr) & (lr < c)).astype(BF16)
    neg_exp_alog = -jnp.exp(alog_ref[...])
    dtb = dtb_ref[...]
    onorm = onorm_ref[...]

    def stack(pieces):
        return pieces[0] if len(pieces) == 1 else jnp.concatenate(pieces, axis=0)

    groups = [[p * g_heads + s for s in range(g_heads)] for p in range(n_groups)]

    n_chunks = tb // c
    par = DN_CHUNKS_PER_ITER if n_chunks % DN_CHUNKS_PER_ITER == 0 else 1
    problems = [(j, hs) for j in range(par) for hs in groups]

    def col(x, r_lo, r_hi, lane):
        return jnp.broadcast_to(x[r_lo:r_hi, lane:lane + 1], (c, LANES))

    def chunk_body(ci, carry):
        r0 = [pl.multiple_of((ci * par + j) * c, c) for j in range(par)]
        ba = [ba_ref[pl.ds(r, c), :] for r in r0]

        def gather(ref, width):
            return [stack([ref[pl.ds(r0[j], c), h * width:(h + 1) * width] for h in hs]) for j, hs in problems]

        qp, kp, vp, zp = gather(q_ref, dk), gather(k_ref, dk), gather(v_ref, DN_DV), gather(z_ref, DN_DV)
        state = [s_ref[h] for h in range(heads)]

        beta_all = [_sigmoid(x) for x in ba]
        g_all = [neg_exp_alog * _softplus(x + dtb) for x in ba]
        gc = [_mm_exact_lhs3(cum_op, g) for g in g_all]
        gct = [x.T for x in gc]
        exp_gc = [jnp.exp(x) for x in gc]

        col_gc = [stack([col(gc[j], 0, c, heads + h) for h in hs]) for j, hs in problems]
        col_eg = [stack([col(exp_gc[j], 0, c, heads + h) for h in hs]) for j, hs in problems]
        col_gl = [stack([col(gc[j], c - 1, c, heads + h) for h in hs]) for j, hs in problems]
        col_beta = [stack([col(beta_all[j], 0, c, h) for h in hs]) for j, hs in problems]
        decay = []
        for i, (j, hs) in enumerate(problems):
            row_gc = gct[j][heads + hs[0]:heads + hs[0] + 1, :]
            for s in range(1, g_heads):
                row_gc = row_gc + pltpu.roll(gct[j][heads + hs[s]:heads + hs[s] + 1, :], s * c, 1)
            decay.append(jnp.exp(jnp.where(causal, col_gc[i] - row_gc, -jnp.inf)))
        n_prob = len(problems)
        kb = [kp[i] * col_beta[i] for i in range(n_prob)]
        kq = [_mm_nt(jnp.concatenate([kb[i], qp[i]], axis=0), kp[i]) for i in range(n_prob)]
        a_low = [jnp.where(strict, kq[i][:LANES] * decay[i], 0.0) for i in range(n_prob)]
        attn = [kq[i][LANES:] * decay[i] for i in range(n_prob)]
        t_inv = [eye - a for a in a_low]
        pw = [_mm3(a, a) for a in a_low]
        for _ in range(int(math.log2(c)) - 2):
            both = [_mm3(jnp.concatenate([t, x], axis=0), x) for t, x in zip(t_inv, pw)]
            t_inv = [t + b[:LANES] for t, b in zip(t_inv, both)]
            pw = [b[LANES:] for b in both]
        t_inv = [t + _mm3(t, x) for t, x in zip(t_inv, pw)]
        uw = [_mm3(t_inv[i], jnp.concatenate([vp[i] * col_beta[i], kb[i] * col_eg[i]], axis=1))
              for i in range(n_prob)]
        qg = [qp[i] * col_eg[i] for i in range(n_prob)]
        kd = [kp[i] * jnp.exp(col_gl[i] - col_gc[i]) for i in range(n_prob)]
        outs = []
        for i, (j, hs) in enumerate(problems):
            u, w = uw[i][:, :DN_DV], uw[i][:, DN_DV:]
            ws, qs = [], []
            for s, h in enumerate(hs):
                wq = _mm(jnp.concatenate([w[s * c:(s + 1) * c], qg[i][s * c:(s + 1) * c]], axis=0), state[h])
                ws.append(wq[:c])
                qs.append(wq[c:])
            v_new = u - stack(ws)
            o = stack(qs) + _mm(attn[i], v_new)
            kd_heads = jnp.concatenate([jnp.where((rows // c) == s, kd[i], 0.0) for s in range(g_heads)], axis=1)
            kv = _mm_tn(kd_heads, v_new)
            for s, h in enumerate(hs):
                decay_last = jnp.exp(gc[j][c - 1:c, heads + h:heads + h + 1])
                state[h] = state[h] * decay_last + kv[s * DN_DK:(s + 1) * DN_DK]
            o = o * lax.rsqrt(jnp.mean(o * o, axis=-1, keepdims=True) + EPS) * onorm
            outs.append(o * (zp[i] * _sigmoid(zp[i])))

        for h in range(heads):
            s_ref[h] = state[h]
        for i, (j, hs) in enumerate(problems):
            for s, h in enumerate(hs):
                o_ref[pl.ds(r0[j], c), h * DN_DV:(h + 1) * DN_DV] = outs[i][s * c:(s + 1) * c].astype(o_ref.dtype)
        return carry

    lax.fori_loop(0, n_chunks // par, chunk_body, 0)

    @pl.when(step == n_steps - 1)
    def _():
        sout_ref[...] = s_ref[...]


def _dn_core(qkv, z, ba, conv_prev, s0, state_layer, layer, conv_w, alog, dtb, onorm,
             *, n_seq, seq_len, row0, tb, c):
    heads = DN_HEADS
    n_steps = seq_len // tb
    blk0 = row0 // tb
    cdim = qkv.shape[1]
    vdim = z.shape[1]
    tok = lambda b, l: (blk0 + b * n_steps + l, 0)
    seq3 = lambda b, l: (b, 0, 0)
    seq4 = lambda b, l: (b, 0, 0, 0)
    fix = layer
    tiles = 2 * tb * (4 * cdim + 4 * vdim + 4 * LANES + 2 * vdim)
    state = 5 * heads * DN_DK * DN_DV * 4 + 4 * SUBLANES * cdim * 4
    scratch = (tb + SUBLANES) * cdim * 4 + 3 * tb * vdim * 4
    temps = 3 * tb * cdim * 4
    return pl.pallas_call(
        functools.partial(_dn_core_kernel, tb=tb, c=c, n_steps=n_steps),
        grid=(n_seq, n_steps),
        in_specs=[
            pl.BlockSpec((tb, cdim), tok), pl.BlockSpec((tb, vdim), tok), pl.BlockSpec((tb, LANES), tok),
            pl.BlockSpec((None, None, SUBLANES, cdim), lambda b, l: (state_layer, b, 0, 0)),
            pl.BlockSpec((None, None, heads, DN_DK, DN_DV), lambda b, l: (state_layer, b, 0, 0, 0)),
            _resident((SUBLANES, cdim), fix), _resident((1, LANES), fix), _resident((1, LANES), fix),
            _resident((1, LANES), fix),
        ],
        out_specs=[
            pl.BlockSpec((tb, vdim), lambda b, l: (b * n_steps + l, 0)),
            pl.BlockSpec((None, SUBLANES, cdim), seq3),
            pl.BlockSpec((None, heads, DN_DK, DN_DV), seq4),
        ],
        out_shape=[
            jax.ShapeDtypeStruct((n_seq * seq_len, vdim), BF16),
            jax.ShapeDtypeStruct((n_seq, SUBLANES, cdim), F32),
            jax.ShapeDtypeStruct((n_seq, heads, DN_DK, DN_DV), F32),
        ],
        scratch_shapes=[
            pltpu.VMEM((tb + SUBLANES, cdim), F32),
            pltpu.VMEM((tb, vdim), F32), pltpu.VMEM((tb, vdim), F32), pltpu.VMEM((tb, vdim), F32),
            pltpu.VMEM((heads, DN_DK, DN_DV), F32),
        ],
        compiler_params=pltpu.CompilerParams(dimension_semantics=("arbitrary", "arbitrary"),
                                             vmem_limit_bytes=_vmem_limit(tiles + state + scratch + temps)),
        name="dn_core",
    )(qkv, z, ba, conv_prev, s0, conv_w, alog, dtb, onorm)


def _rope128(x, cos, sin_lo, sin_hi):
    half = ROPE_DIM // 2
    return x * cos + pltpu.roll(x, LANES - half, 1) * sin_lo + pltpu.roll(x, half, 1) * sin_hi


def _mla_proj_kernel(yp_ref, ys_ref, win_ref, qn_ref, kvn_ref, wqn_ref, wqr_ref, wuk_ref, wuv_ref,
                     cos_ref, sl_ref, sh_ref, q_ref, k_ref, v_ref, ckvp_ref, ckvs_ref, krp_ref, krs_ref,
                     *, n_prompt_tiles):
    heads = MLA_HEADS

    def project(y_ref, ckv_ref, kr_ref):
        proj = _mm(y_ref[...], win_ref[...])
        cq = _rms_norm(proj[:, :Q_LORA], qn_ref[...]).astype(BF16)
        ckv = _rms_norm(proj[:, Q_LORA:Q_LORA + KV_LORA], kvn_ref[...])
        cos, sl, sh = cos_ref[...], sl_ref[...], sh_ref[...]
        kr = _rope128(proj[:, Q_LORA + KV_LORA:], cos, sl, sh)
        ckv_ref[...] = ckv
        kr_ref[...] = kr[:, :ROPE_DIM]
        ckvb = ckv.astype(BF16)
        qn = _mm(cq, wqn_ref[...])
        qr = _mm(cq, wqr_ref[...])
        kn = _mm(ckvb, wuk_ref[...])
        v_ref[...] = _mm(ckvb, wuv_ref[...]).astype(BF16)
        krb = kr.astype(BF16)
        for h in range(heads):
            lo = h * QK_PAD
            q_ref[:, lo:lo + LANES] = (qn[:, h * LANES:(h + 1) * LANES] * Q_PRESCALE).astype(BF16)
            q_ref[:, lo + LANES:lo + QK_PAD] = (
                _rope128(qr[:, h * LANES:(h + 1) * LANES], cos, sl, sh) * Q_PRESCALE).astype(BF16)
            k_ref[:, lo:lo + LANES] = kn[:, h * LANES:(h + 1) * LANES].astype(BF16)
            k_ref[:, lo + LANES:lo + QK_PAD] = krb

    _on_active_stream(n_prompt_tiles, project, (yp_ref, ckvp_ref, krp_ref), (ys_ref, ckvs_ref, krs_ref))


def _mla_proj(y, layer, win, qn, kvn, wqn, wqr, wuk, wuv, cos, sl, sh, *, tm, seq_len):
    (tp, d), ts = y[0].shape, y[1].shape[0]
    t = tp + ts
    heads = MLA_HEADS
    nin = win.shape[2]
    n_p = tp // tm
    seq_tiles = seq_len // tm
    row = lambda i: (i, 0)
    table = lambda i: (jnp.where(i < n_p, i % seq_tiles, seq_tiles + i - n_p), 0)
    fix = layer
    weights = 2 * (d * nin + Q_LORA * 2 * heads * LANES + KV_LORA * 2 * heads * LANES)
    tiles = 2 * tm * (2 * 4 * d + 3 * 4 * LANES + 2 * 2 * heads * QK_PAD + 2 * heads * V_DIM
                      + 2 * 4 * KV_LORA + 2 * 4 * LANES)
    temps = tm * (4 * nin + 4 * 4 * heads * LANES + 2 * d)
    return pl.pallas_call(
        functools.partial(_mla_proj_kernel, n_prompt_tiles=n_p),
        grid=(t // tm,),
        in_specs=[
            *_stream_specs(tm, d, n_p), _resident((d, nin), fix), _resident((1, Q_LORA), fix),
            _resident((1, KV_LORA), fix), _resident((Q_LORA, heads * LANES), fix),
            _resident((Q_LORA, heads * LANES), fix), _resident((KV_LORA, heads * NOPE_DIM), fix),
            _resident((KV_LORA, heads * V_DIM), fix),
            pl.BlockSpec((tm, LANES), table), pl.BlockSpec((tm, LANES), table), pl.BlockSpec((tm, LANES), table),
        ],
        out_specs=[
            pl.BlockSpec((tm, heads * QK_PAD), row), pl.BlockSpec((tm, heads * QK_PAD), row),
            pl.BlockSpec((tm, heads * V_DIM), row),
            *_stream_specs(tm, KV_LORA, n_p), *_stream_specs(tm, ROPE_DIM, n_p),
        ],
        out_shape=[
            jax.ShapeDtypeStruct((t, heads * QK_PAD), BF16), jax.ShapeDtypeStruct((t, heads * QK_PAD), BF16),
            jax.ShapeDtypeStruct((t, heads * V_DIM), BF16),
            jax.ShapeDtypeStruct((tp, KV_LORA), F32), jax.ShapeDtypeStruct((ts, KV_LORA), F32),
            jax.ShapeDtypeStruct((tp, ROPE_DIM), F32), jax.ShapeDtypeStruct((ts, ROPE_DIM), F32),
        ],
        compiler_params=pltpu.CompilerParams(dimension_semantics=("arbitrary",),
                                             vmem_limit_bytes=_vmem_limit(weights + tiles + temps)),
        name="mla_proj",
    )(*y, win, qn, kvn, wqn, wqr, wuk, wuv, cos, sl, sh)


def _flash_kernel(q_ref, k_ref, v_ref, o_ref, m_ref, l_ref, acc_ref, *, tq, tk):
    i = pl.program_id(2)
    n_sub = tq // tk
    reps = tk // LANES

    def online_softmax(stats, q, kj, vj, mask):
        m_prev, l_prev, acc = stats
        s = _mm_nt(q, kj)
        if mask is not None:
            s = jnp.where(mask, s, -jnp.inf)
        m_new = jnp.maximum(m_prev, jnp.max(s, axis=-1, keepdims=True))
        alpha = jnp.exp2(m_prev - m_new)
        p = jnp.exp2(s - jnp.concatenate([m_new] * reps, axis=1))
        l_new = alpha * l_prev + jnp.sum(p, axis=-1, keepdims=True)
        return m_new, l_new, alpha * acc + _mm(p, vj)

    m_ref[...] = jnp.full(m_ref.shape, -jnp.inf, F32)
    l_ref[...] = jnp.zeros(l_ref.shape, F32)
    acc_ref[...] = jnp.zeros(acc_ref.shape, F32)

    def body(j, carry):
        q = q_ref[...]
        tiles = []
        for u in range(n_sub):
            k0 = pl.multiple_of((j * n_sub + u) * tk, tk)
            tiles.append((k_ref[pl.ds(k0, tk), :], v_ref[pl.ds(k0, tk), :]))
        stats = (m_ref[...], l_ref[...], acc_ref[...])
        for kj, vj in tiles:
            stats = online_softmax(stats, q, kj, vj, None)
        m_ref[...], l_ref[...], acc_ref[...] = stats
        return carry

    lax.fori_loop(0, i, body, 0)

    for u in range(n_sub):
        k0 = pl.multiple_of(i * tq + u * tk, tk)
        rows = slice(u * tk, tq)
        nr = tq - u * tk
        qc = (lax.broadcasted_iota(jnp.int32, (nr, tk), 0) + u * tk) // CHUNK
        kc = (lax.broadcasted_iota(jnp.int32, (nr, tk), 1) + u * tk) // CHUNK
        stats = (m_ref[rows, :], l_ref[rows, :], acc_ref[rows, :])
        stats = online_softmax(stats, q_ref[rows, :], k_ref[pl.ds(k0, tk), :], v_ref[pl.ds(k0, tk), :], kc <= qc)
        m_ref[rows, :], l_ref[rows, :], acc_ref[rows, :] = stats
    o_ref[...] = (acc_ref[...] / l_ref[...]).astype(o_ref.dtype)


def _flash_attention(q, k, v, *, n_seq, seq_len, tq, tk):
    heads = MLA_HEADS
    nq = seq_len // tq
    blocks = 2 * (tq * QK_PAD * 2 + seq_len * QK_PAD * 2 + seq_len * V_DIM * 2 + tq * V_DIM * 2)
    scratch = 3 * tq * LANES * 4
    temps = (tq // tk + 2) * tq * tk * 4
    return pl.pallas_call(
        functools.partial(_flash_kernel, tq=tq, tk=tk),
        grid=(n_seq, heads, nq),
        in_specs=[
            pl.BlockSpec((tq, QK_PAD), lambda b, h, i: (b * nq + i, h)),
            pl.BlockSpec((seq_len, QK_PAD), lambda b, h, i: (b, h)),
            pl.BlockSpec((seq_len, V_DIM), lambda b, h, i: (b, h)),
        ],
        out_specs=pl.BlockSpec((tq, V_DIM), lambda b, h, i: (b * nq + i, h)),
        out_shape=jax.ShapeDtypeStruct((n_seq * seq_len, heads * V_DIM), BF16),
        scratch_shapes=[pltpu.VMEM((tq, LANES), F32), pltpu.VMEM((tq, LANES), F32), pltpu.VMEM((tq, V_DIM), F32)],
        compiler_params=pltpu.CompilerParams(dimension_semantics=("arbitrary", "arbitrary", "arbitrary"),
                                             vmem_limit_bytes=_vmem_limit(blocks + scratch + temps)),
        name="mla_flash",
    )(q, k, v)


def _decode_attn_kernel(q_ref, ckv_c_ref, kr_c_ref, ckv_n_ref, kr_n_ref, wuk_ref, wuv_ref, o_ref, *, past, seq):
    heads = MLA_HEADS
    qall = q_ref[...]
    qlat, qrope = [], []
    for h in range(heads):
        qn = qall[:, h * QK_PAD:h * QK_PAD + LANES]
        qlat.append(_mm_nt(qn, wuk_ref[h]).astype(BF16))
        qrope.append(qall[:, h * QK_PAD + LANES:(h + 1) * QK_PAD])
    qlat = jnp.concatenate(qlat, axis=0)
    qrope = jnp.concatenate(qrope, axis=0)
    n_rows = heads * seq
    n_keys = past + seq
    pad = (-n_keys) % LANES
    ckv_parts = [ckv_c_ref[...].astype(BF16), ckv_n_ref[...].astype(BF16)]
    kr_parts = [kr_c_ref[...].astype(BF16), kr_n_ref[...].astype(BF16)]
    if pad:
        ckv_parts.append(jnp.zeros((pad, KV_LORA), BF16))
        kr_parts.append(jnp.zeros((pad, ROPE_DIM), BF16))
    ckv_all = jnp.concatenate(ckv_parts, axis=0)
    kr_all = jnp.concatenate(kr_parts, axis=0)
    kr_all = jnp.concatenate([kr_all, jnp.zeros((n_keys + pad, LANES - ROPE_DIM), BF16)], axis=1)
    s = _mm_nt(qlat, ckv_all) + _mm_nt(qrope, kr_all)
    q_pos = past + lax.broadcasted_iota(jnp.int32, (n_rows, n_keys + pad), 0) % seq
    k_pos = lax.broadcasted_iota(jnp.int32, (n_rows, n_keys + pad), 1)
    visible = ((k_pos // CHUNK) <= (q_pos // CHUNK)) & (k_pos < n_keys)
    s = jnp.where(visible, s, -jnp.inf)
    p = jnp.exp2(s - jnp.max(s, axis=-1, keepdims=True))
    p = p / jnp.sum(p, axis=-1, keepdims=True)
    o_lat = _mm(p, ckv_all)
    for h in range(heads):
        o_ref[:, h * V_DIM:(h + 1) * V_DIM] = _mm(o_lat[h * seq:(h + 1) * seq], wuv_ref[h]).astype(o_ref.dtype)


def _decode_attention(q, layer, ckv_cache, kr_cache, ckv, kr, wuk_h, wuv_h, *, n_seq, seq, row0):
    heads = MLA_HEADS
    past = ckv_cache.shape[2]
    blk0 = row0 // seq
    tok = lambda b: (blk0 + b, 0)
    new = lambda b: (b, 0)
    fix3 = layer
    blocks = 2 * (seq * heads * QK_PAD * 2 + past * (KV_LORA + LANES) * 4 + seq * (KV_LORA + LANES) * 4
                  + seq * heads * V_DIM * 2) + 2 * 2 * heads * KV_LORA * LANES * 2
    temps = (past + seq + LANES) * (KV_LORA + LANES) * 2 * 2 + 6 * heads * seq * (past + seq + LANES) * 4
    return pl.pallas_call(
        functools.partial(_decode_attn_kernel, past=past, seq=seq),
        grid=(n_seq,),
        in_specs=[
            pl.BlockSpec((seq, heads * QK_PAD), tok),
            pl.BlockSpec((None, None, past, KV_LORA), lambda b: (layer, b, 0, 0)),
            pl.BlockSpec((None, None, past, ROPE_DIM), lambda b: (layer, b, 0, 0)),
            pl.BlockSpec((seq, KV_LORA), new), pl.BlockSpec((seq, ROPE_DIM), new),
            _resident((heads, KV_LORA, NOPE_DIM), fix3), _resident((heads, KV_LORA, V_DIM), fix3),
        ],
        out_specs=pl.BlockSpec((seq, heads * V_DIM), lambda b: (b, 0)),
        out_shape=jax.ShapeDtypeStruct((n_seq * seq, heads * V_DIM), BF16),
        compiler_params=pltpu.CompilerParams(dimension_semantics=("arbitrary",),
                                             vmem_limit_bytes=_vmem_limit(blocks + temps)),
        name="mla_decode",
    )(q, ckv_cache, kr_cache, ckv, kr, wuk_h, wuv_h)


def _largest_tile(limit, *sizes):
    t = limit
    while any(s % t for s in sizes):
        t //= 2
    return t


def _rope_tables(pos):
    half = ROPE_DIM // 2
    inv_freq = ROPE_THETA ** (-jnp.arange(half, dtype=F32) / half)
    ang = pos.astype(F32)[:, None] * inv_freq[None, :]
    cos, sin = jnp.cos(ang), jnp.sin(ang)
    zero = jnp.zeros_like(cos)
    return (jnp.concatenate([cos, cos, zero, zero], axis=1), jnp.concatenate([-sin, zero, zero, zero], axis=1),
            jnp.concatenate([zero, sin, zero, zero], axis=1))


def kernel(x_prompt, x_sample, state_dn_conv, state_dn_recurrent, cache_mla_ckv, cache_mla_krope, p_prompt, p_sample, ln1_g, ln1_b, ln2_g, ln2_b, mlp_w_up, mlp_w_down, ple_w_proj, ple_norm, ple_w_gate, dn_w_in, dn_conv_w, dn_a_log, dn_dt_bias, dn_o_norm, dn_w_o, mla_w_in, mla_q_norm, mla_w_uq, mla_kv_norm, mla_w_uk, mla_w_uv, mla_w_o):
    bp, lp, d = x_prompt.shape
    bs, ls, _ = x_sample.shape
    depth = ln1_g.shape[0]
    past = cache_mla_ckv.shape[2]
    tp, ts = bp * lp, bs * ls
    alpha = (2 * depth) ** 0.25
    tm = _largest_tile(TOKEN_TILE, tp, ts)
    heads = DN_HEADS
    qk_dim = heads * DN_DK
    conv_dim = 2 * qk_dim + heads * DN_DV
    v_dim = heads * DN_DV

    y = (x_prompt.reshape(tp, d), x_sample.reshape(ts, d))
    p = (p_prompt.reshape(depth * tp, -1), p_sample.reshape(depth * ts, -1))

    rows3 = lambda a: a.reshape(a.shape[0], 1, -1)
    bf = lambda a: a.astype(BF16)
    ln1g, ln1b, ln2g, ln2b, pn = rows3(ln1_g), rows3(ln1_b), rows3(ln2_g), rows3(ln2_b), rows3(ple_norm)
    wup, wdn, wpp, wg = bf(mlp_w_up), bf(mlp_w_down), bf(ple_w_proj), bf(ple_w_gate)
    dn_wo, mla_wo = bf(dn_w_o), bf(mla_w_o)

    dn_wqkv = bf(dn_w_in[:, :, :conv_dim])
    dn_wz = bf(dn_w_in[:, :, conv_dim:conv_dim + v_dim])
    dn_wba = bf(jnp.pad(dn_w_in[:, :, conv_dim + v_dim:], ((0, 0), (0, 0), (0, LANES - 2 * heads))))
    dn_cw = jnp.pad(dn_conv_w, ((0, 0), (0, SUBLANES - CONV_W), (0, 0)))
    gate_lanes = lambda a: rows3(jnp.pad(a, ((0, 0), (heads, LANES - 2 * heads))))
    dn_alog, dn_dtb, dn_onorm = gate_lanes(dn_a_log), gate_lanes(dn_dt_bias), rows3(dn_o_norm)
    conv_hist = jnp.pad(state_dn_conv, ((0, 0), (0, 0), (SUBLANES - (CONV_W - 1), 0), (0, 0)))
    zero_hist = jnp.zeros((1, bp, SUBLANES, conv_dim), F32)
    zero_state = jnp.zeros((1, bp, heads, DN_DK, DN_DV), F32)

    n_mla = mla_w_in.shape[0]
    mla_win = bf(jnp.pad(mla_w_in, ((0, 0), (0, 0), (0, LANES - ROPE_DIM))))
    w_uq = mla_w_uq.reshape(n_mla, Q_LORA, MLA_HEADS, NOPE_DIM + ROPE_DIM)
    mla_wqn = bf(w_uq[..., :NOPE_DIM].reshape(n_mla, Q_LORA, MLA_HEADS * NOPE_DIM))
    mla_wqr = bf(jnp.pad(w_uq[..., NOPE_DIM:], ((0, 0), (0, 0), (0, 0), (0, LANES - ROPE_DIM))
                         ).reshape(n_mla, Q_LORA, MLA_HEADS * LANES))
    wuk, wuv = bf(mla_w_uk), bf(mla_w_uv)
    mla_wuk = wuk.reshape(n_mla, KV_LORA, MLA_HEADS * NOPE_DIM)
    mla_wuv = wuv.reshape(n_mla, KV_LORA, MLA_HEADS * V_DIM)
    mla_wuk_h, mla_wuv_h = jnp.transpose(wuk, (0, 2, 1, 3)), jnp.transpose(wuv, (0, 2, 1, 3))
    mla_qn, mla_kvn = rows3(mla_q_norm), rows3(mla_kv_norm)
    pos = jnp.concatenate([jnp.arange(lp), jnp.tile(past + jnp.arange(ls), bs)])
    cos_t, sin_lo_t, sin_hi_t = _rope_tables(pos)

    p_conv, p_rec, p_ckv, p_kr = [], [], [], []
    s_conv, s_rec, s_ckv, s_kr = [], [], [], []
    for i in range(depth):
        j = i // N_MIXERS
        if i % N_MIXERS == 0:
            qkv, z, ba = _dn_proj(y, j, dn_wqkv, dn_wz, dn_wba, tm=tm)
            params = (j, dn_cw, dn_alog, dn_dtb, dn_onorm)
            o_p, ct_p, st_p = _dn_core(qkv, z, ba, zero_hist, zero_state, 0, *params, n_seq=bp, seq_len=lp, row0=0,
                                       tb=_largest_tile(DN_STEP_TOKENS, lp), c=min(CHUNK, lp))
            o_s, ct_s, st_s = _dn_core(qkv, z, ba, conv_hist, state_dn_recurrent, j, *params, n_seq=bs, seq_len=ls,
                                       row0=tp, tb=ls, c=min(CHUNK, ls))
            p_conv.append(ct_p[:, SUBLANES - (CONV_W - 1):])
            p_rec.append(st_p)
            s_conv.append(ct_s[:, SUBLANES - (CONV_W - 1):])
            s_rec.append(st_s)
            w_o = dn_wo
        else:
            q, k, v, ckv_p, ckv_s, kr_p, kr_s = _mla_proj(
                y, j, mla_win, mla_qn, mla_kvn, mla_wqn, mla_wqr, mla_wuk, mla_wuv,
                cos_t, sin_lo_t, sin_hi_t, tm=tm, seq_len=lp)
            tq = _largest_tile(ATTN_Q_TILE, lp)
            o_p = _flash_attention(q, k, v, n_seq=bp, seq_len=lp, tq=tq, tk=min(ATTN_K_TILE, tq))
            o_s = _decode_attention(q, j, cache_mla_ckv, cache_mla_krope, ckv_s, kr_s, mla_wuk_h, mla_wuv_h,
                                    n_seq=bs, seq=ls, row0=tp)
            p_ckv.append(ckv_p.reshape(bp, lp, KV_LORA))
            p_kr.append(kr_p.reshape(bp, lp, ROPE_DIM))
            s_ckv.append(ckv_s.reshape(bs, ls, KV_LORA))
            s_kr.append(kr_s.reshape(bs, ls, ROPE_DIM))
            w_o = mla_wo
        y = _finish_layer(y, (o_p, o_s), p, i, w_o, j, ln1g, ln1b, wup, wdn, ln2g, ln2b, wpp, pn, wg,
                          alpha=alpha, tm=tm)
    return (y[0].reshape(bp, lp, d), y[1].reshape(bs, ls, d),
            jnp.stack(p_conv), jnp.stack(p_rec), jnp.stack(p_ckv), jnp.stack(p_kr),
            jnp.stack(s_conv), jnp.stack(s_rec), jnp.stack(s_ckv), jnp.stack(s_kr))
```

```python
import functools
import math

import jax
import jax.numpy as jnp
from jax import lax
from jax.experimental import pallas as pl
from jax.experimental.pallas import tpu as pltpu

F32 = jnp.float32
BF16 = jnp.bfloat16

CHUNK = 64
N_MIXERS = 2
EPS = 1e-6
DN_HEADS = 8
DN_DK = 128
DN_DV = 128
CONV_W = 4
MLA_HEADS = 8
Q_LORA = 512
KV_LORA = 256
NOPE_DIM = 128
ROPE_DIM = 64
V_DIM = 128
MLA_SCALE = (NOPE_DIM + ROPE_DIM) ** -0.5
ROPE_THETA = 10000.0
Q_PRESCALE = MLA_SCALE * math.log2(math.e)

LANES = 128
SUBLANES = 8
VMEM_BUDGET_BYTES = 60000 * 1024

TOKEN_TILE = 512
FF_CHUNK = 1024
DN_STEP_TOKENS = 256
ATTN_Q_TILE = 2048
ATTN_K_TILE = 512
QK_PAD = 2 * LANES


def _vmem_limit(nbytes):
    return int(min(VMEM_BUDGET_BYTES, nbytes))


def _nbytes(shape, dtype):
    return math.prod(shape) * jnp.dtype(dtype).itemsize


def _resident(shape, layer):
    index = (layer,) + (0,) * len(shape)
    return pl.BlockSpec((None, *shape), lambda *_: index, pipeline_mode=pl.Buffered(1))


def _mm(a, b):
    return jnp.dot(a.astype(BF16), b.astype(BF16), preferred_element_type=F32)


def _mm_nt(a, b):
    return lax.dot_general(a.astype(BF16), b.astype(BF16), (((1,), (1,)), ((), ())),
                           preferred_element_type=F32)


def _mm_tn(a, b):
    return pl.dot(a.astype(BF16), b.astype(BF16), trans_a=True)


def _split2(x):
    hi = x.astype(BF16)
    lo = (x - hi.astype(F32)).astype(BF16)
    return hi, lo


def _mm3(a, b):
    ah, al = _split2(a)
    bh, bl = _split2(b)
    n = b.shape[1]
    rhs = jnp.concatenate([jnp.concatenate([bh, bl], axis=1),
                           jnp.concatenate([bh, jnp.zeros_like(bl)], axis=1)], axis=0)
    r = jnp.dot(jnp.concatenate([ah, al], axis=1), rhs, preferred_element_type=F32)
    return r[:, :n] + r[:, n:]


def _mm_exact_lhs3(a_bf16, b):
    b0 = b.astype(BF16)
    r1 = b - b0.astype(F32)
    b1 = r1.astype(BF16)
    b2 = (r1 - b1.astype(F32)).astype(BF16)
    return _mm(a_bf16, b0) + (_mm(a_bf16, b1) + _mm(a_bf16, b2))


def _layer_norm(x, g, b):
    xc = x - jnp.mean(x, axis=-1, keepdims=True)
    var = jnp.mean(xc * xc, axis=-1, keepdims=True)
    return xc * lax.rsqrt(var + EPS) * g + b


def _rms_norm(x, g):
    return x * lax.rsqrt(jnp.mean(x * x, axis=-1, keepdims=True) + EPS) * g


def _sigmoid(x):
    return 1.0 / (1.0 + jnp.exp(-x))


def _softplus(x):
    return jnp.maximum(x, 0.0) + jnp.log1p(jnp.exp(-jnp.abs(x)))


def _stream_specs(tm, width, n_prompt_tiles, first_tile=0):
    prompt = pl.BlockSpec((tm, width), lambda i: (first_tile + jnp.minimum(i, n_prompt_tiles - 1), 0))
    sample = pl.BlockSpec((tm, width), lambda i: (jnp.maximum(i - n_prompt_tiles, 0), 0))
    return prompt, sample


def _on_active_stream(n_prompt_tiles, body, prompt_refs, sample_refs):
    on_prompt = pl.program_id(0) < n_prompt_tiles

    @pl.when(on_prompt)
    def _():
        body(*prompt_refs)

    @pl.when(jnp.logical_not(on_prompt))
    def _():
        body(*sample_refs)


def _finish_kernel(yp_ref, ys_ref, op_ref, os_ref, pp_ref, ps_ref, wo_ref, ln1g_ref, ln1b_ref, wup_ref, wdn_ref,
                   ln2g_ref, ln2b_ref, wpp_ref, pn_ref, wg_ref, outp_ref, outs_ref, *, alpha, d_ff, n_prompt_tiles):
    def tail(y_ref, o_ref, p_ref, out_ref):
        m = _mm(o_ref[...], wo_ref[...])
        y1 = _layer_norm(alpha * y_ref[...] + m, ln1g_ref[...], ln1b_ref[...])
        y1b = y1.astype(BF16)
        acc = jnp.zeros_like(y1)
        for c in range(d_ff // FF_CHUNK):
            h = _mm(y1b, wup_ref[:, c * FF_CHUNK:(c + 1) * FF_CHUNK])
            h = jnp.square(jnp.maximum(h, 0.0))
            acc = acc + _mm(h, wdn_ref[c * FF_CHUNK:(c + 1) * FF_CHUNK, :])
        y2 = _layer_norm(alpha * y1 + acc, ln2g_ref[...], ln2b_ref[...])
        e = _rms_norm(_mm(p_ref[...], wpp_ref[...]), pn_ref[...])
        gate = _sigmoid(_mm(y2, wg_ref[...]))
        out_ref[...] = y2 + gate * e

    _on_active_stream(n_prompt_tiles, tail, (yp_ref, op_ref, pp_ref, outp_ref), (ys_ref, os_ref, ps_ref, outs_ref))


def _finish_layer(y, o, p, layer, wo, wo_layer, ln1g, ln1b, wup, wdn, ln2g, ln2b, wpp, pn, wg, *, alpha, tm):
    (tp, d), ts = y[0].shape, y[1].shape[0]
    d_ff = wup.shape[2]
    ple = p[0].shape[1]
    n_p, n_s = tp // tm, ts // tm
    fix = layer
    p_prompt_spec, _ = _stream_specs(tm, ple, n_p, first_tile=layer * n_p)
    p_sample_spec = pl.BlockSpec((tm, ple), lambda i: (layer * n_s + jnp.maximum(i - n_p, 0), 0))
    weights = 2 * (d * d * 2 + d * d_ff + d_ff * d + ple * d)
    tiles = 2 * 2 * (2 * tm * d * 4 + tm * d * 2 + tm * ple * 4)
    temps = 6 * tm * d * 4 + 2 * tm * FF_CHUNK * 4
    return pl.pallas_call(
        functools.partial(_finish_kernel, alpha=alpha, d_ff=d_ff, n_prompt_tiles=n_p),
        grid=(n_p + n_s,),
        in_specs=[
            *_stream_specs(tm, d, n_p), *_stream_specs(tm, d, n_p), p_prompt_spec, p_sample_spec,
            _resident((d, d), wo_layer), _resident((1, d), fix), _resident((1, d), fix),
            _resident((d, d_ff), fix), _resident((d_ff, d), fix), _resident((1, d), fix), _resident((1, d), fix),
            _resident((ple, d), fix), _resident((1, d), fix), _resident((d, d), fix),
        ],
        out_specs=list(_stream_specs(tm, d, n_p)),
        out_shape=[jax.ShapeDtypeStruct((tp, d), F32), jax.ShapeDtypeStruct((ts, d), F32)],
        compiler_params=pltpu.CompilerParams(dimension_semantics=("arbitrary",),
                                             vmem_limit_bytes=_vmem_limit(weights + tiles + temps)),
        name="finish_layer",
    )(*y, *o, *p, wo, ln1g, ln1b, wup, wdn, ln2g, ln2b, wpp, pn, wg)


def _dn_proj_kernel(yp_ref, ys_ref, wqkv_ref, wz_ref, wba_ref, qkv_ref, z_ref, ba_ref, *, n_prompt_tiles):
    def project(y_ref):
        yb = y_ref[...].astype(BF16)
        qkv_ref[...] = _mm(yb, wqkv_ref[...])
        z_ref[...] = _mm(yb, wz_ref[...])
        ba_ref[...] = _mm(yb, wba_ref[...])

    _on_active_stream(n_prompt_tiles, project, (yp_ref,), (ys_ref,))


def _dn_proj(y, layer, wqkv, wz, wba, *, tm):
    (tp, d), ts = y[0].shape, y[1].shape[0]
    t = tp + ts
    nq, nz, nb = wqkv.shape[2], wz.shape[2], wba.shape[2]
    row = lambda i: (i, 0)
    fix = layer
    weights = 2 * d * (nq + nz + nb)
    tiles = 2 * 4 * tm * (2 * d + nq + nz + nb)
    temps = 2 * tm * d + 4 * tm * nq
    return pl.pallas_call(
        functools.partial(_dn_proj_kernel, n_prompt_tiles=tp // tm),
        grid=(t // tm,),
        in_specs=[*_stream_specs(tm, d, tp // tm), _resident((d, nq), fix), _resident((d, nz), fix),
                  _resident((d, nb), fix)],
        out_specs=[pl.BlockSpec((tm, nq), row), pl.BlockSpec((tm, nz), row), pl.BlockSpec((tm, nb), row)],
        out_shape=[jax.ShapeDtypeStruct((t, nq), F32), jax.ShapeDtypeStruct((t, nz), F32),
                   jax.ShapeDtypeStruct((t, nb), F32)],
        compiler_params=pltpu.CompilerParams(dimension_semantics=("arbitrary",),
                                             vmem_limit_bytes=_vmem_limit(weights + tiles + temps)),
        name="dn_proj",
    )(*y, wqkv, wz, wba)


def _dn_core_kernel(qkv_ref, z_ref, ba_ref, cprev_ref, s0_ref, cw_ref, alog_ref, dtb_ref, onorm_ref,
                    o_ref, ctail_ref, sout_ref,
                    xs_ref, s_ref, *, tb, c, n_steps):
    heads = DN_HEADS
    dk = DN_DK
    qk_dim = heads * dk
    g_heads = LANES // c
    n_groups = heads // g_heads
    step = pl.program_id(1)

    @pl.when(step == 0)
    def _():
        xs_ref[0:SUBLANES, :] = cprev_ref[...]
        s_ref[...] = s0_ref[...]

    x = qkv_ref[...]
    xs_ref[SUBLANES:SUBLANES + tb, :] = x
    cw = cw_ref[...]
    conv = x * cw[CONV_W - 1:CONV_W, :]
    for k in range(1, CONV_W):
        conv = conv + xs_ref[SUBLANES - k:SUBLANES - k + tb, :] * cw[CONV_W - 1 - k:CONV_W - k, :]
    tail = xs_ref[tb:tb + SUBLANES, :]
    xs_ref[0:SUBLANES, :] = tail

    act = conv * _sigmoid(conv)
    q_heads, k_heads, v_heads = [], [], []
    for h in range(heads):
        qh = act[:, h * dk:(h + 1) * dk]
        kh = act[:, qk_dim + h * dk:qk_dim + (h + 1) * dk]
        q_heads.append(qh * lax.rsqrt(jnp.sum(qh * qh, axis=-1, keepdims=True) + EPS) * (dk ** -0.5))
        k_heads.append(kh * lax.rsqrt(jnp.sum(kh * kh, axis=-1, keepdims=True) + EPS))
        v_heads.append(act[:, 2 * qk_dim + h * DN_DV:2 * qk_dim + (h + 1) * DN_DV])

    rows = lax.broadcasted_iota(jnp.int32, (LANES, LANES), 0)
    cols = lax.broadcasted_iota(jnp.int32, (LANES, LANES), 1)
    same_block = (rows // c) == (cols // c)
    causal = same_block & (rows >= cols)
    strict = same_block & (rows > cols)
    eye = (rows == cols).astype(F32)
    lr = lax.broadcasted_iota(jnp.int32, (LANES, c), 0)
    lc = lax.broadcasted_iota(jnp.int32, (LANES, c), 1)
    cum_op = ((lc <= lr) & (lr < c)).astype(BF16)
    neg_exp_alog = -jnp.exp(alog_ref[...])
    dtb = dtb_ref[...]
    onorm = onorm_ref[...]

    def stack(pieces):
        return pieces[0] if len(pieces) == 1 else jnp.concatenate(pieces, axis=0)

    groups = [[p * g_heads + s for s in range(g_heads)] for p in range(n_groups)]

    n_chunks = tb // c
    problems = [(j, hs) for j in range(n_chunks) for hs in groups]

    def col(x, r_lo, r_hi, lane):
        return jnp.broadcast_to(x[r_lo:r_hi, lane:lane + 1], (c, LANES))

    def all_chunks():
        r0 = [j * c for j in range(n_chunks)]
        ba = [ba_ref[r:r + c, :] for r in r0]

        def gather(per_head):
            return [stack([per_head[h][r0[j]:r0[j] + c] for h in hs]) for j, hs in problems]

        qp, kp, vp = gather(q_heads), gather(k_heads), gather(v_heads)
        zp = [stack([z_ref[r0[j]:r0[j] + c, h * DN_DV:(h + 1) * DN_DV] for h in hs]) for j, hs in problems]
        state = [s_ref[h] for h in range(heads)]

        beta_all = [_sigmoid(x) for x in ba]
        g_all = [neg_exp_alog * _softplus(x + dtb) for x in ba]
        gc = [_mm_exact_lhs3(cum_op, g) for g in g_all]
        gct = [x.T for x in gc]
        exp_gc = [jnp.exp(x) for x in gc]

        col_gc = [stack([col(gc[j], 0, c, heads + h) for h in hs]) for j, hs in problems]
        col_eg = [stack([col(exp_gc[j], 0, c, heads + h) for h in hs]) for j, hs in problems]
        col_gl = [stack([col(gc[j], c - 1, c, heads + h) for h in hs]) for j, hs in problems]
        col_beta = [stack([col(beta_all[j], 0, c, h) for h in hs]) for j, hs in problems]
        decay = []
        for i, (j, hs) in enumerate(problems):
            row_gc = gct[j][heads + hs[0]:heads + hs[0] + 1, :]
            for s in range(1, g_heads):
                row_gc = row_gc + pltpu.roll(gct[j][heads + hs[s]:heads + hs[s] + 1, :], s * c, 1)
            decay.append(jnp.exp(jnp.where(causal, col_gc[i] - row_gc, -jnp.inf)))
        n_prob = len(problems)
        kb = [kp[i] * col_beta[i] for i in range(n_prob)]
        kq = [_mm_nt(jnp.concatenate([kb[i], qp[i]], axis=0), kp[i]) for i in range(n_prob)]
        a_low = [jnp.where(strict, kq[i][:LANES] * decay[i], 0.0) for i in range(n_prob)]
        attn = [kq[i][LANES:] * decay[i] for i in range(n_prob)]
        t_inv = [eye - a for a in a_low]
        pw = [_mm3(a, a) for a in a_low]
        for _ in range(int(math.log2(c)) - 2):
            both = [_mm3(jnp.concatenate([t, x], axis=0), x) for t, x in zip(t_inv, pw)]
            t_inv = [t + b[:LANES] for t, b in zip(t_inv, both)]
            pw = [b[LANES:] for b in both]
        t_inv = [t + _mm3(t, x) for t, x in zip(t_inv, pw)]
        uw = [_mm3(t_inv[i], jnp.concatenate([vp[i] * col_beta[i], kb[i] * col_eg[i]], axis=1))
              for i in range(n_prob)]
        qg = [qp[i] * col_eg[i] for i in range(n_prob)]
        kd = [kp[i] * jnp.exp(col_gl[i] - col_gc[i]) for i in range(n_prob)]
        outs = []
        for i, (j, hs) in enumerate(problems):
            u, w = uw[i][:, :DN_DV], uw[i][:, DN_DV:]
            ws, qs = [], []
            for s, h in enumerate(hs):
                wq = _mm(jnp.concatenate([w[s * c:(s + 1) * c], qg[i][s * c:(s + 1) * c]], axis=0), state[h])
                ws.append(wq[:c])
                qs.append(wq[c:])
            v_new = u - stack(ws)
            o = stack(qs) + _mm(attn[i], v_new)
            kd_heads = jnp.concatenate([jnp.where((rows // c) == s, kd[i], 0.0) for s in range(g_heads)], axis=1)
            kv = _mm_tn(kd_heads, v_new)
            for s, h in enumerate(hs):
                decay_last = jnp.exp(gc[j][c - 1:c, heads + h:heads + h + 1])
                state[h] = state[h] * decay_last + kv[s * DN_DK:(s + 1) * DN_DK]
            o = o * lax.rsqrt(jnp.mean(o * o, axis=-1, keepdims=True) + EPS) * onorm
            outs.append(o * (zp[i] * _sigmoid(zp[i])))

        for h in range(heads):
            s_ref[h] = state[h]
        for i, (j, hs) in enumerate(problems):
            for s, h in enumerate(hs):
                o_ref[r0[j]:r0[j] + c, h * DN_DV:(h + 1) * DN_DV] = outs[i][s * c:(s + 1) * c].astype(o_ref.dtype)

    all_chunks()

    @pl.when(step == n_steps - 1)
    def _():
        ctail_ref[...] = xs_ref[0:SUBLANES, :]
        sout_ref[...] = s_ref[...]


def _dn_core(qkv, z, ba, conv_prev, s0, state_layer, layer, conv_w, alog, dtb, onorm,
             *, n_seq, seq_len, row0, tb, c):
    heads = DN_HEADS
    n_steps = seq_len // tb
    blk0 = row0 // tb
    cdim = qkv.shape[1]
    vdim = z.shape[1]
    tok = lambda b, l: (blk0 + b * n_steps + l, 0)
    seq3 = lambda b, l: (b, 0, 0)
    seq4 = lambda b, l: (b, 0, 0, 0)
    fix = layer
    tiles = 2 * tb * (4 * cdim + 4 * vdim + 4 * LANES + 2 * vdim)
    state = 5 * heads * DN_DK * DN_DV * 4 + 4 * SUBLANES * cdim * 4
    scratch = (tb + SUBLANES) * cdim * 4
    temps = 6 * tb * cdim * 4
    return pl.pallas_call(
        functools.partial(_dn_core_kernel, tb=tb, c=c, n_steps=n_steps),
        grid=(n_seq, n_steps),
        in_specs=[
            pl.BlockSpec((tb, cdim), tok), pl.BlockSpec((tb, vdim), tok), pl.BlockSpec((tb, LANES), tok),
            pl.BlockSpec((None, None, SUBLANES, cdim), lambda b, l: (state_layer, b, 0, 0)),
            pl.BlockSpec((None, None, heads, DN_DK, DN_DV), lambda b, l: (state_layer, b, 0, 0, 0)),
            _resident((SUBLANES, cdim), fix), _resident((1, LANES), fix), _resident((1, LANES), fix),
            _resident((1, LANES), fix),
        ],
        out_specs=[
            pl.BlockSpec((tb, vdim), lambda b, l: (b * n_steps + l, 0)),
            pl.BlockSpec((None, SUBLANES, cdim), seq3),
            pl.BlockSpec((None, heads, DN_DK, DN_DV), seq4),
        ],
        out_shape=[
            jax.ShapeDtypeStruct((n_seq * seq_len, vdim), BF16),
            jax.ShapeDtypeStruct((n_seq, SUBLANES, cdim), F32),
            jax.ShapeDtypeStruct((n_seq, heads, DN_DK, DN_DV), F32),
        ],
        scratch_shapes=[
            pltpu.VMEM((tb + SUBLANES, cdim), F32),
            pltpu.VMEM((heads, DN_DK, DN_DV), F32),
        ],
        compiler_params=pltpu.CompilerParams(dimension_semantics=("arbitrary", "arbitrary"),
                                             vmem_limit_bytes=_vmem_limit(tiles + state + scratch + temps)),
        name="dn_core",
    )(qkv, z, ba, conv_prev, s0, conv_w, alog, dtb, onorm)


def _rope128(x, cos, sin_lo, sin_hi):
    half = ROPE_DIM // 2
    return x * cos + pltpu.roll(x, LANES - half, 1) * sin_lo + pltpu.roll(x, half, 1) * sin_hi


def _mla_proj_kernel(yp_ref, ys_ref, win_ref, qn_ref, kvn_ref, wqn_ref, wqr_ref, wuk_ref, wuv_ref,
                     cos_ref, sl_ref, sh_ref, q_ref, k_ref, v_ref, ckvp_ref, ckvs_ref, krp_ref, krs_ref,
                     *, n_prompt_tiles):
    heads = MLA_HEADS

    def project(y_ref, ckv_ref, kr_ref):
        proj = _mm(y_ref[...], win_ref[...])
        cq = _rms_norm(proj[:, :Q_LORA], qn_ref[...]).astype(BF16)
        ckv = _rms_norm(proj[:, Q_LORA:Q_LORA + KV_LORA], kvn_ref[...])
        cos, sl, sh = cos_ref[...], sl_ref[...], sh_ref[...]
        kr = _rope128(proj[:, Q_LORA + KV_LORA:], cos, sl, sh)
        ckv_ref[...] = ckv
        kr_ref[...] = kr[:, :ROPE_DIM]
        ckvb = ckv.astype(BF16)
        qn = _mm(cq, wqn_ref[...])
        qr = _mm(cq, wqr_ref[...])
        kn = _mm(ckvb, wuk_ref[...])
        v_ref[...] = _mm(ckvb, wuv_ref[...]).astype(BF16)
        krb = kr.astype(BF16)
        for h in range(heads):
            lo = h * QK_PAD
            q_ref[:, lo:lo + LANES] = (qn[:, h * LANES:(h + 1) * LANES] * Q_PRESCALE).astype(BF16)
            q_ref[:, lo + LANES:lo + QK_PAD] = (
                _rope128(qr[:, h * LANES:(h + 1) * LANES], cos, sl, sh) * Q_PRESCALE).astype(BF16)
            k_ref[:, lo:lo + LANES] = kn[:, h * LANES:(h + 1) * LANES].astype(BF16)
            k_ref[:, lo + LANES:lo + QK_PAD] = krb

    _on_active_stream(n_prompt_tiles, project, (yp_ref, ckvp_ref, krp_ref), (ys_ref, ckvs_ref, krs_ref))


def _mla_proj(y, layer, win, qn, kvn, wqn, wqr, wuk, wuv, cos, sl, sh, *, tm, seq_len):
    (tp, d), ts = y[0].shape, y[1].shape[0]
    t = tp + ts
    heads = MLA_HEADS
    nin = win.shape[2]
    n_p = tp // tm
    seq_tiles = seq_len // tm
    row = lambda i: (i, 0)
    table = lambda i: (jnp.where(i < n_p, i % seq_tiles, seq_tiles + i - n_p), 0)
    fix = layer
    weights = 2 * (d * nin + Q_LORA * 2 * heads * LANES + KV_LORA * 2 * heads * LANES)
    tiles = 2 * tm * (2 * 4 * d + 3 * 4 * LANES + 2 * 2 * heads * QK_PAD + 2 * heads * V_DIM
                      + 2 * 4 * KV_LORA + 2 * 4 * LANES)
    temps = tm * (4 * nin + 4 * 4 * heads * LANES + 2 * d)
    return pl.pallas_call(
        functools.partial(_mla_proj_kernel, n_prompt_tiles=n_p),
        grid=(t // tm,),
        in_specs=[
            *_stream_specs(tm, d, n_p), _resident((d, nin), fix), _resident((1, Q_LORA), fix),
            _resident((1, KV_LORA), fix), _resident((Q_LORA, heads * LANES), fix),
            _resident((Q_LORA, heads * LANES), fix), _resident((KV_LORA, heads * NOPE_DIM), fix),
            _resident((KV_LORA, heads * V_DIM), fix),
            pl.BlockSpec((tm, LANES), table), pl.BlockSpec((tm, LANES), table), pl.BlockSpec((tm, LANES), table),
        ],
        out_specs=[
            pl.BlockSpec((tm, heads * QK_PAD), row), pl.BlockSpec((tm, heads * QK_PAD), row),
            pl.BlockSpec((tm, heads * V_DIM), row),
            *_stream_specs(tm, KV_LORA, n_p), *_stream_specs(tm, ROPE_DIM, n_p),
        ],
        out_shape=[
            jax.ShapeDtypeStruct((t, heads * QK_PAD), BF16), jax.ShapeDtypeStruct((t, heads * QK_PAD), BF16),
            jax.ShapeDtypeStruct((t, heads * V_DIM), BF16),
            jax.ShapeDtypeStruct((tp, KV_LORA), F32), jax.ShapeDtypeStruct((ts, KV_LORA), F32),
            jax.ShapeDtypeStruct((tp, ROPE_DIM), F32), jax.ShapeDtypeStruct((ts, ROPE_DIM), F32),
        ],
        compiler_params=pltpu.CompilerParams(dimension_semantics=("arbitrary",),
                                             vmem_limit_bytes=_vmem_limit(weights + tiles + temps)),
        name="mla_proj",
    )(*y, win, qn, kvn, wqn, wqr, wuk, wuv, cos, sl, sh)


def _flash_kernel(q_ref, k_ref, v_ref, o_ref, m_ref, l_ref, acc_ref, *, tq, tk):
    i = pl.program_id(2)
    n_sub = tq // tk
    reps = tk // LANES

    def online_softmax(stats, q, kj, vj, mask):
        m_prev, l_prev, acc = stats
        s = _mm_nt(q, kj)
        if mask is not None:
            s = jnp.where(mask, s, -jnp.inf)
        m_new = jnp.maximum(m_prev, jnp.max(s, axis=-1, keepdims=True))
        alpha = jnp.exp2(m_prev - m_new)
        p = jnp.exp2(s - jnp.concatenate([m_new] * reps, axis=1))
        l_new = alpha * l_prev + jnp.sum(p, axis=-1, keepdims=True)
        return m_new, l_new, alpha * acc + _mm(p, vj)

    m_ref[...] = jnp.full(m_ref.shape, -jnp.inf, F32)
    l_ref[...] = jnp.zeros(l_ref.shape, F32)
    acc_ref[...] = jnp.zeros(acc_ref.shape, F32)

    def body(j, carry):
        q = q_ref[...]
        tiles = []
        for u in range(n_sub):
            k0 = pl.multiple_of((j * n_sub + u) * tk, tk)
            tiles.append((k_ref[pl.ds(k0, tk), :], v_ref[pl.ds(k0, tk), :]))
        stats = (m_ref[...], l_ref[...], acc_ref[...])
        for kj, vj in tiles:
            stats = online_softmax(stats, q, kj, vj, None)
        m_ref[...], l_ref[...], acc_ref[...] = stats
        return carry

    lax.fori_loop(0, i, body, 0)

    for u in range(n_sub):
        k0 = pl.multiple_of(i * tq + u * tk, tk)
        rows = slice(u * tk, tq)
        nr = tq - u * tk
        qc = (lax.broadcasted_iota(jnp.int32, (nr, tk), 0) + u * tk) // CHUNK
        kc = (lax.broadcasted_iota(jnp.int32, (nr, tk), 1) + u * tk) // CHUNK
        stats = (m_ref[rows, :], l_ref[rows, :], acc_ref[rows, :])
        stats = online_softmax(stats, q_ref[rows, :], k_ref[pl.ds(k0, tk), :], v_ref[pl.ds(k0, tk), :], kc <= qc)
        m_ref[rows, :], l_ref[rows, :], acc_ref[rows, :] = stats
    o_ref[...] = (acc_ref[...] / l_ref[...]).astype(o_ref.dtype)


def _flash_attention(q, k, v, *, n_seq, seq_len, tq, tk):
    heads = MLA_HEADS
    nq = seq_len // tq
    blocks = 2 * (tq * QK_PAD * 2 + seq_len * QK_PAD * 2 + seq_len * V_DIM * 2 + tq * V_DIM * 2)
    scratch = 3 * tq * LANES * 4
    temps = (tq // tk + 2) * tq * tk * 4
    return pl.pallas_call(
        functools.partial(_flash_kernel, tq=tq, tk=tk),
        grid=(n_seq, heads, nq),
        in_specs=[
            pl.BlockSpec((tq, QK_PAD), lambda b, h, i: (b * nq + i, h)),
            pl.BlockSpec((seq_len, QK_PAD), lambda b, h, i: (b, h)),
            pl.BlockSpec((seq_len, V_DIM), lambda b, h, i: (b, h)),
        ],
        out_specs=pl.BlockSpec((tq, V_DIM), lambda b, h, i: (b * nq + i, h)),
        out_shape=jax.ShapeDtypeStruct((n_seq * seq_len, heads * V_DIM), BF16),
        scratch_shapes=[pltpu.VMEM((tq, LANES), F32), pltpu.VMEM((tq, LANES), F32), pltpu.VMEM((tq, V_DIM), F32)],
        compiler_params=pltpu.CompilerParams(dimension_semantics=("arbitrary", "arbitrary", "arbitrary"),
                                             vmem_limit_bytes=_vmem_limit(blocks + scratch + temps)),
        name="mla_flash",
    )(q, k, v)


def _decode_attn_kernel(q_ref, ckv_c_ref, kr_c_ref, ckv_n_ref, kr_n_ref, wuk_ref, wuv_ref, o_ref, *, past, seq):
    heads = MLA_HEADS
    qall = q_ref[...]
    qlat, qrope = [], []
    for h in range(heads):
        qn = qall[:, h * QK_PAD:h * QK_PAD + LANES]
        qlat.append(_mm_nt(qn, wuk_ref[h]).astype(BF16))
        qrope.append(qall[:, h * QK_PAD + LANES:(h + 1) * QK_PAD])
    qlat = jnp.concatenate(qlat, axis=0)
    qrope = jnp.concatenate(qrope, axis=0)
    n_rows = heads * seq
    n_keys = past + seq
    pad = (-n_keys) % LANES
    ckv_parts = [ckv_c_ref[...].astype(BF16), ckv_n_ref[...].astype(BF16)]
    kr_parts = [kr_c_ref[...].astype(BF16), kr_n_ref[...].astype(BF16)]
    if pad:
        ckv_parts.append(jnp.zeros((pad, KV_LORA), BF16))
        kr_parts.append(jnp.zeros((pad, ROPE_DIM), BF16))
    ckv_all = jnp.concatenate(ckv_parts, axis=0)
    kr_all = jnp.concatenate(kr_parts, axis=0)
    kr_all = jnp.concatenate([kr_all, jnp.zeros((n_keys + pad, LANES - ROPE_DIM), BF16)], axis=1)
    s = _mm_nt(qlat, ckv_all) + _mm_nt(qrope, kr_all)
    q_pos = past + lax.broadcasted_iota(jnp.int32, (n_rows, n_keys + pad), 0) % seq
    k_pos = lax.broadcasted_iota(jnp.int32, (n_rows, n_keys + pad), 1)
    visible = ((k_pos // CHUNK) <= (q_pos // CHUNK)) & (k_pos < n_keys)
    s = jnp.where(visible, s, -jnp.inf)
    p = jnp.exp2(s - jnp.max(s, axis=-1, keepdims=True))
    p = p / jnp.sum(p, axis=-1, keepdims=True)
    o_lat = _mm(p, ckv_all)
    for h in range(heads):
        o_ref[:, h * V_DIM:(h + 1) * V_DIM] = _mm(o_lat[h * seq:(h + 1) * seq], wuv_ref[h]).astype(o_ref.dtype)


def _decode_attention(q, layer, ckv_cache, kr_cache, ckv, kr, wuk_h, wuv_h, *, n_seq, seq, row0):
    heads = MLA_HEADS
    past = ckv_cache.shape[2]
    blk0 = row0 // seq
    tok = lambda b: (blk0 + b, 0)
    new = lambda b: (b, 0)
    fix3 = layer
    blocks = 2 * (seq * heads * QK_PAD * 2 + past * (KV_LORA + LANES) * 4 + seq * (KV_LORA + LANES) * 4
                  + seq * heads * V_DIM * 2) + 2 * 2 * heads * KV_LORA * LANES * 2
    temps = (past + seq + LANES) * (KV_LORA + LANES) * 2 * 2 + 6 * heads * seq * (past + seq + LANES) * 4
    return pl.pallas_call(
        functools.partial(_decode_attn_kernel, past=past, seq=seq),
        grid=(n_seq,),
        in_specs=[
            pl.BlockSpec((seq, heads * QK_PAD), tok),
            pl.BlockSpec((None, None, past, KV_LORA), lambda b: (layer, b, 0, 0)),
            pl.BlockSpec((None, None, past, ROPE_DIM), lambda b: (layer, b, 0, 0)),
            pl.BlockSpec((seq, KV_LORA), new), pl.BlockSpec((seq, ROPE_DIM), new),
            _resident((heads, KV_LORA, NOPE_DIM), fix3), _resident((heads, KV_LORA, V_DIM), fix3),
        ],
        out_specs=pl.BlockSpec((seq, heads * V_DIM), lambda b: (b, 0)),
        out_shape=jax.ShapeDtypeStruct((n_seq * seq, heads * V_DIM), BF16),
        compiler_params=pltpu.CompilerParams(dimension_semantics=("arbitrary",),
                                             vmem_limit_bytes=_vmem_limit(blocks + temps)),
        name="mla_decode",
    )(q, ckv_cache, kr_cache, ckv, kr, wuk_h, wuv_h)


def _largest_tile(limit, *sizes):
    t = limit
    while any(s % t for s in sizes):
        t //= 2
    return t


def _rope_tables(pos):
    half = ROPE_DIM // 2
    inv_freq = ROPE_THETA ** (-jnp.arange(half, dtype=F32) / half)
    ang = pos.astype(F32)[:, None] * inv_freq[None, :]
    cos, sin = jnp.cos(ang), jnp.sin(ang)
    zero = jnp.zeros_like(cos)
    return (jnp.concatenate([cos, cos, zero, zero], axis=1), jnp.concatenate([-sin, zero, zero, zero], axis=1),
            jnp.concatenate([zero, sin, zero, zero], axis=1))


def kernel(x_prompt, x_sample, state_dn_conv, state_dn_recurrent, cache_mla_ckv, cache_mla_krope, p_prompt, p_sample, ln1_g, ln1_b, ln2_g, ln2_b, mlp_w_up, mlp_w_down, ple_w_proj, ple_norm, ple_w_gate, dn_w_in, dn_conv_w, dn_a_log, dn_dt_bias, dn_o_norm, dn_w_o, mla_w_in, mla_q_norm, mla_w_uq, mla_kv_norm, mla_w_uk, mla_w_uv, mla_w_o):
    bp, lp, d = x_prompt.shape
    bs, ls, _ = x_sample.shape
    depth = ln1_g.shape[0]
    past = cache_mla_ckv.shape[2]
    tp, ts = bp * lp, bs * ls
    alpha = (2 * depth) ** 0.25
    tm = _largest_tile(TOKEN_TILE, tp, ts)
    heads = DN_HEADS
    qk_dim = heads * DN_DK
    conv_dim = 2 * qk_dim + heads * DN_DV
    v_dim = heads * DN_DV

    y = (x_prompt.reshape(tp, d), x_sample.reshape(ts, d))
    p = (p_prompt.reshape(depth * tp, -1), p_sample.reshape(depth * ts, -1))

    rows3 = lambda a: a.reshape(a.shape[0], 1, -1)
    bf = lambda a: a.astype(BF16)
    ln1g, ln1b, ln2g, ln2b, pn = rows3(ln1_g), rows3(ln1_b), rows3(ln2_g), rows3(ln2_b), rows3(ple_norm)
    wup, wdn, wpp, wg = bf(mlp_w_up), bf(mlp_w_down), bf(ple_w_proj), bf(ple_w_gate)
    dn_wo, mla_wo = bf(dn_w_o), bf(mla_w_o)

    dn_wqkv = bf(dn_w_in[:, :, :conv_dim])
    dn_wz = bf(dn_w_in[:, :, conv_dim:conv_dim + v_dim])
    dn_wba = bf(jnp.pad(dn_w_in[:, :, conv_dim + v_dim:], ((0, 0), (0, 0), (0, LANES - 2 * heads))))
    dn_cw = jnp.pad(dn_conv_w, ((0, 0), (0, SUBLANES - CONV_W), (0, 0)))
    gate_lanes = lambda a: rows3(jnp.pad(a, ((0, 0), (heads, LANES - 2 * heads))))
    dn_alog, dn_dtb, dn_onorm = gate_lanes(dn_a_log), gate_lanes(dn_dt_bias), rows3(dn_o_norm)
    conv_hist = jnp.pad(state_dn_conv, ((0, 0), (0, 0), (SUBLANES - (CONV_W - 1), 0), (0, 0)))
    zero_hist = jnp.zeros((1, bp, SUBLANES, conv_dim), F32)
    zero_state = jnp.zeros((1, bp, heads, DN_DK, DN_DV), F32)

    n_mla = mla_w_in.shape[0]
    mla_win = bf(jnp.pad(mla_w_in, ((0, 0), (0, 0), (0, LANES - ROPE_DIM))))
    w_uq = mla_w_uq.reshape(n_mla, Q_LORA, MLA_HEADS, NOPE_DIM + ROPE_DIM)
    mla_wqn = bf(w_uq[..., :NOPE_DIM].reshape(n_mla, Q_LORA, MLA_HEADS * NOPE_DIM))
    mla_wqr = bf(jnp.pad(w_uq[..., NOPE_DIM:], ((0, 0), (0, 0), (0, 0), (0, LANES - ROPE_DIM))
                         ).reshape(n_mla, Q_LORA, MLA_HEADS * LANES))
    wuk, wuv = bf(mla_w_uk), bf(mla_w_uv)
    mla_wuk = wuk.reshape(n_mla, KV_LORA, MLA_HEADS * NOPE_DIM)
    mla_wuv = wuv.reshape(n_mla, KV_LORA, MLA_HEADS * V_DIM)
    mla_wuk_h, mla_wuv_h = jnp.transpose(wuk, (0, 2, 1, 3)), jnp.transpose(wuv, (0, 2, 1, 3))
    mla_qn, mla_kvn = rows3(mla_q_norm), rows3(mla_kv_norm)
    pos = jnp.concatenate([jnp.arange(lp), jnp.tile(past + jnp.arange(ls), bs)])
    cos_t, sin_lo_t, sin_hi_t = _rope_tables(pos)

    p_conv, p_rec, p_ckv, p_kr = [], [], [], []
    s_conv, s_rec, s_ckv, s_kr = [], [], [], []
    for i in range(depth):
        j = i // N_MIXERS
        if i % N_MIXERS == 0:
            qkv, z, ba = _dn_proj(y, j, dn_wqkv, dn_wz, dn_wba, tm=tm)
            params = (j, dn_cw, dn_alog, dn_dtb, dn_onorm)
            o_p, ct_p, st_p = _dn_core(qkv, z, ba, zero_hist, zero_state, 0, *params, n_seq=bp, seq_len=lp, row0=0,
                                       tb=_largest_tile(DN_STEP_TOKENS, lp), c=min(CHUNK, lp))
            o_s, ct_s, st_s = _dn_core(qkv, z, ba, conv_hist, state_dn_recurrent, j, *params, n_seq=bs, seq_len=ls,
                                       row0=tp, tb=ls, c=min(CHUNK, ls))
            p_conv.append(ct_p[:, SUBLANES - (CONV_W - 1):])
            p_rec.append(st_p)
            s_conv.append(ct_s[:, SUBLANES - (CONV_W - 1):])
            s_rec.append(st_s)
            w_o = dn_wo
        else:
            q, k, v, ckv_p, ckv_s, kr_p, kr_s = _mla_proj(
                y, j, mla_win, mla_qn, mla_kvn, mla_wqn, mla_wqr, mla_wuk, mla_wuv,
                cos_t, sin_lo_t, sin_hi_t, tm=tm, seq_len=lp)
            tq = _largest_tile(ATTN_Q_TILE, lp)
            o_p = _flash_attention(q, k, v, n_seq=bp, seq_len=lp, tq=tq, tk=min(ATTN_K_TILE, tq))
            o_s = _decode_attention(q, j, cache_mla_ckv, cache_mla_krope, ckv_s, kr_s, mla_wuk_h, mla_wuv_h,
                                    n_seq=bs, seq=ls, row0=tp)
            p_ckv.append(ckv_p.reshape(bp, lp, KV_LORA))
            p_kr.append(kr_p.reshape(bp, lp, ROPE_DIM))
            s_ckv.append(ckv_s.reshape(bs, ls, KV_LORA))
            s_kr.append(kr_s.reshape(bs, ls, ROPE_DIM))
            w_o = mla_wo
        y = _finish_layer(y, (o_p, o_s), p, i, w_o, j, ln1g, ln1b, wup, wdn, ln2g, ln2b, wpp, pn, wg,
                          alpha=alpha, tm=tm)
    return (y[0].reshape(bp, lp, d), y[1].reshape(bs, ls, d),
            jnp.stack(p_conv), jnp.stack(p_rec), jnp.stack(p_ckv), jnp.stack(p_kr),
            jnp.stack(s_conv), jnp.stack(s_rec), jnp.stack(s_ckv), jnp.stack(s_kr))
```
